```python
import math
import jax, jax.numpy as jnp
from jax import lax
import numpy as np

D_MODEL = 2048
BATCH = 4
SEQ = 8192
DEPTH = 4

ATTN_WIDTH = D_MODEL // 2
HEAD_DIM = 64
N_ATTN_HEADS = ATTN_WIDTH // HEAD_DIM
ATTN_PATTERNS = ((128, 1), (512, 4), (2048, 16))
SSM_WIDTH = D_MODEL - ATTN_WIDTH
SSM_GROUP = 16
SSM_STATE = 64
N_SSM_GROUPS = SSM_WIDTH // SSM_GROUP
IN_WIDTH = 3 * ATTN_WIDTH + SSM_WIDTH
D_FF = 4 * D_MODEL
DEEPNORM_ALPHA = (2 * DEPTH) ** 0.25
DEEPNORM_BETA = (8 * DEPTH) ** -0.25
LN_EPS = 1e-5
NEG_BIG = -1e30
STEP_MIN = 1e-3
STEP_MAX = 1e-1

kernel_name = 'hymba_s5_longnet_deepnorm_encoder'


def _layer_norm(x, g, b):
    xf = x.astype(jnp.float32)
    mu = xf.mean(-1, keepdims=True)
    var = jnp.square(xf - mu).mean(-1, keepdims=True)
    y = (xf - mu) * lax.rsqrt(var + LN_EPS) * g.astype(jnp.float32) + b.astype(jnp.float32)
    return y.astype(x.dtype)


def _rms_norm(x, g):
    xf = x.astype(jnp.float32)
    y = xf * lax.rsqrt(jnp.square(xf).mean(-1, keepdims=True) + LN_EPS) * g.astype(jnp.float32)
    return y.astype(x.dtype)


def _alibi_slopes(n_heads):
    return jnp.exp2(-8.0 * jnp.arange(1, n_heads + 1, dtype=jnp.float32) / n_heads)


def _dilated_band_attention(q, k, v, window, dilation):
    bsz, nh, seq, hd = q.shape
    half = window // (2 * dilation)
    length = seq // dilation
    nb = -(-length // half)
    lp = nb * half

    def strided(t):
        return t.reshape(bsz, nh, length, dilation, hd).transpose(0, 1, 3, 2, 4)

    qs, ks, vs = strided(q), strided(k), strided(v)
    qb = jnp.pad(qs, ((0, 0), (0, 0), (0, 0), (0, lp - length), (0, 0))).reshape(
        bsz, nh, dilation, nb, half, hd)

    def band(t):
        tp = jnp.pad(t, ((0, 0), (0, 0), (0, 0), (half, lp - length + half), (0, 0))).reshape(
            bsz, nh, dilation, nb + 2, half, hd)
        return jnp.concatenate([tp[:, :, :, 0:nb], tp[:, :, :, 1:nb + 1], tp[:, :, :, 2:nb + 2]], axis=-2)

    kb, vb = band(ks), band(vs)
    q_pos = jnp.arange(nb)[:, None] * half + jnp.arange(half)[None, :]
    k_pos = jnp.arange(nb)[:, None] * half - half + jnp.arange(3 * half)[None, :]
    rel = jnp.abs(k_pos[:, None, :] - q_pos[:, :, None])
    valid = (rel <= half) & (k_pos[:, None, :] >= 0) & (k_pos[:, None, :] < length)
    slopes = _alibi_slopes(nh)
    bias = -slopes[:, None, None, None, None] * (rel * dilation).astype(jnp.float32)

    scores = jnp.einsum('bhrnqd,bhrnkd->bhrnqk', qb, kb).astype(jnp.float32) * (hd ** -0.5) + bias
    scores = jnp.where(valid, scores, NEG_BIG)
    m = scores.max(-1, keepdims=True)
    p = jnp.exp(scores - m)
    den = p.sum(-1)
    out = jnp.einsum('bhrnqk,bhrnkd->bhrnqd', p, vb.astype(jnp.float32)) / den[..., None]
    lse = m[..., 0] + jnp.log(den)
    out = out.reshape(bsz, nh, dilation, lp, hd)[:, :, :, :length].transpose(0, 1, 3, 2, 4)
    lse = lse.reshape(bsz, nh, dilation, lp)[..., :length].transpose(0, 1, 3, 2)
    return out.reshape(bsz, nh, seq, hd), lse.reshape(bsz, nh, seq)


def _dilated_mixture_attention(q, k, v):
    outs, lses = [], []
    for window, dilation in ATTN_PATTERNS:
        o, l = _dilated_band_attention(q, k, v, window, dilation)
        outs.append(o)
        lses.append(l)
    w = jax.nn.softmax(jnp.stack(lses), axis=0)
    return jnp.einsum('pbhs,pbhsd->bhsd', w, jnp.stack(outs))


def _complex_affine_combine(e1, e2):
    a1r, a1i, b1r, b1i = e1
    a2r, a2i, b2r, b2i = e2
    return (a1r * a2r - a1i * a2i,
            a1r * a2i + a1i * a2r,
            a2r * b1r - a2i * b1i + b2r,
            a2r * b1i + a2i * b1r + b2i)


def _s5_bidirectional(u, lam_re, lam_im, log_step, b_re, b_im, c_re, c_im, d_skip):
    bsz, seq, _ = u.shape
    f32 = jnp.float32
    ug = u.reshape(bsz, seq, N_SSM_GROUPS, SSM_GROUP).astype(f32)
    y = ug * d_skip.reshape(N_SSM_GROUPS, SSM_GROUP).astype(f32)
    for direction in range(2):
        step = jnp.exp(log_step[direction].astype(f32))[:, None]
        lr, li = lam_re[direction].astype(f32), lam_im[direction].astype(f32)
        mag = jnp.exp(lr * step)
        a_re, a_im = mag * jnp.cos(li * step), mag * jnp.sin(li * step)
        den = lr * lr + li * li
        coef_re = ((a_re - 1.0) * lr + a_im * li) / den
        coef_im = (a_im * lr - (a_re - 1.0) * li) / den
        br, bi = b_re[direction].astype(f32), b_im[direction].astype(f32)
        bb_re = coef_re[..., None] * br - coef_im[..., None] * bi
        bb_im = coef_re[..., None] * bi + coef_im[..., None] * br
        bu_re = jnp.einsum('bsgc,gpc->bsgp', ug, bb_re)
        bu_im = jnp.einsum('bsgc,gpc->bsgp', ug, bb_im)
        shape = bu_re.shape
        elems = (jnp.broadcast_to(a_re, shape), jnp.broadcast_to(a_im, shape), bu_re, bu_im)
        _, _, x_re, x_im = lax.associative_scan(_complex_affine_combine, elems,
                                                reverse=(direction == 1), axis=1)
        y = (y + jnp.einsum('bsgp,gcp->bsgc', x_re, c_re[direction].astype(f32))
             - jnp.einsum('bsgp,gcp->bsgc', x_im, c_im[direction].astype(f32)))
    return y.reshape(bsz, seq, SSM_WIDTH).astype(u.dtype)


def _hybrid_mixer(h, w_in, lam_re, lam_im, log_step, b_re, b_im, c_re, c_im, d_skip,
                  w_glu, b_glu, g_attn, g_ssm, w_out):
    bsz, seq, _ = h.shape
    proj = h @ w_in
    q = proj[..., :ATTN_WIDTH]
    k = proj[..., ATTN_WIDTH:2 * ATTN_WIDTH]
    v = proj[..., 2 * ATTN_WIDTH:3 * ATTN_WIDTH]
    u = proj[..., 3 * ATTN_WIDTH:]

    def heads(t):
        return t.reshape(bsz, seq, N_ATTN_HEADS, HEAD_DIM).transpose(0, 2, 1, 3)

    attn = _dilated_mixture_attention(heads(q), heads(k), heads(v))
    attn = attn.transpose(0, 2, 1, 3).reshape(bsz, seq, ATTN_WIDTH).astype(h.dtype)

    ssm = jax.nn.gelu(_s5_bidirectional(u, lam_re, lam_im, log_step, b_re, b_im, c_re, c_im, d_skip))
    ssm = ssm * jax.nn.sigmoid(ssm @ w_glu + b_glu)

    merged = jnp.concatenate([_rms_norm(attn, g_attn), _rms_norm(ssm, g_ssm)], axis=-1)
    return merged @ w_out


def setup_inputs(seed: int = 0) -> dict:
    key = jax.random.key(seed)
    ks = jax.random.split(key, 24)
    f32 = jnp.float32

    def nrm(k, shape, s):
        return s * jax.random.normal(k, shape, f32)

    lam_shape = (DEPTH, 2, N_SSM_GROUPS, SSM_STATE)
    return {
        'x': nrm(ks[0], (BATCH, SEQ, D_MODEL), 1.0),
        'c': nrm(ks[1], (BATCH, D_MODEL), 1.0),
        'w_ada': nrm(ks[2], (DEPTH, D_MODEL, 6 * D_MODEL), 0.1 * D_MODEL ** -0.5),
        'b_ada': nrm(ks[3], (DEPTH, 6 * D_MODEL), 0.01),
        'w_in': nrm(ks[4], (DEPTH, D_MODEL, IN_WIDTH), D_MODEL ** -0.5),
        'ssm_lam_re': -0.5 * jnp.exp(nrm(ks[5], lam_shape, 0.05)),
        'ssm_lam_im': jnp.pi * jnp.arange(SSM_STATE, dtype=f32) + nrm(ks[6], lam_shape, 0.01),
        'ssm_log_step': jax.random.uniform(ks[7], (DEPTH, 2, N_SSM_GROUPS), f32,
                                           math.log(STEP_MIN), math.log(STEP_MAX)),
        'ssm_b_re': nrm(ks[8], (DEPTH, 2, N_SSM_GROUPS, SSM_STATE, SSM_GROUP), 0.5),
        'ssm_b_im': nrm(ks[9], (DEPTH, 2, N_SSM_GROUPS, SSM_STATE, SSM_GROUP), 0.5),
        'ssm_c_re': nrm(ks[10], (DEPTH, 2, N_SSM_GROUPS, SSM_GROUP, SSM_STATE), (2 * SSM_STATE) ** -0.5),
        'ssm_c_im': nrm(ks[11], (DEPTH, 2, N_SSM_GROUPS, SSM_GROUP, SSM_STATE), (2 * SSM_STATE) ** -0.5),
        'ssm_d': nrm(ks[12], (DEPTH, SSM_WIDTH), 1.0),
        'w_glu': nrm(ks[13], (DEPTH, SSM_WIDTH, SSM_WIDTH), SSM_WIDTH ** -0.5),
        'b_glu': nrm(ks[14], (DEPTH, SSM_WIDTH), 0.01),
        'g_attn': 1.0 + nrm(ks[15], (DEPTH, ATTN_WIDTH), 0.02),
        'g_ssm': 1.0 + nrm(ks[16], (DEPTH, SSM_WIDTH), 0.02),
        'w_out': nrm(ks[17], (DEPTH, D_MODEL, D_MODEL), DEEPNORM_BETA * D_MODEL ** -0.5),
        'ln1_g': 1.0 + nrm(ks[18], (DEPTH, D_MODEL), 0.02),
        'ln1_b': nrm(ks[19], (DEPTH, D_MODEL), 0.02),
        'w_mlp1': nrm(ks[20], (DEPTH, D_MODEL, D_FF), D_MODEL ** -0.5),
        'w_mlp2': nrm(ks[21], (DEPTH, D_FF, D_MODEL), DEEPNORM_BETA * D_FF ** -0.5),
        'ln2_g': 1.0 + nrm(ks[22], (DEPTH, D_MODEL), 0.02),
        'ln2_b': nrm(ks[23], (DEPTH, D_MODEL), 0.02),
    }


def reference(x, c, w_ada, b_ada, w_in, ssm_lam_re, ssm_lam_im, ssm_log_step, ssm_b_re, ssm_b_im,
              ssm_c_re, ssm_c_im, ssm_d, w_glu, b_glu, g_attn, g_ssm, w_out, ln1_g, ln1_b,
              w_mlp1, w_mlp2, ln2_g, ln2_b):
    cond = jax.nn.silu(c)
    for layer in range(DEPTH):
        mods = cond @ w_ada[layer] + b_ada[layer]
        sh1, sc1, g1, sh2, sc2, g2 = jnp.split(mods[:, None, :], 6, axis=-1)
        h = x * (1.0 + sc1) + sh1
        mix = _hybrid_mixer(h, w_in[layer], ssm_lam_re[layer], ssm_lam_im[layer], ssm_log_step[layer],
                            ssm_b_re[layer], ssm_b_im[layer], ssm_c_re[layer], ssm_c_im[layer],
                            ssm_d[layer], w_glu[layer], b_glu[layer], g_attn[layer], g_ssm[layer],
                            w_out[layer])
        x = _layer_norm(DEEPNORM_ALPHA * x + (1.0 + g1) * mix, ln1_g[layer], ln1_b[layer])
        h = x * (1.0 + sc2) + sh2
        ff = jnp.square(jax.nn.relu(h @ w_mlp1[layer])) @ w_mlp2[layer]
        x = _layer_norm(DEEPNORM_ALPHA * x + (1.0 + g2) * ff, ln2_g[layer], ln2_b[layer])
    return x
```

```python
import functools

import jax
import jax.numpy as jnp
from jax import lax
from jax.experimental import pallas as pl
from jax.experimental.pallas import tpu as pltpu

F32 = jnp.float32
BF16 = jnp.bfloat16

LANES = 128
SUBLANES = 8
VMEM_LIMIT_BYTES = 56 * 1024 * 1024

HEAD_DIM = 64
ATTN_DILATIONS = (1, 4, 16)
ATTN_HALF = 64
ATTN_Q = 128
ATTN_K = ATTN_Q + 2 * ATTN_HALF
ATTN_TILE = ATTN_Q * max(ATTN_DILATIONS)
SSM_GROUP = 16
S5_GROUPS_PER_STEP = LANES // SSM_GROUP
S5_SUB = 64
S5_BLOCK = SUBLANES * S5_SUB
LN_EPS = 1e-5
NEG_BIG = -1e30

IN_PROJ_ROWS = 512
MIX_ROWS = 256
MLP_ROWS = 512
MLP_FF_CHUNK = 512
ADA_COLS = 1024


def _cparams(*sem):
    return pltpu.CompilerParams(dimension_semantics=sem, vmem_limit_bytes=VMEM_LIMIT_BYTES)


def _layer_norm(y, g, b):
    mu = jnp.mean(y, axis=-1, keepdims=True)
    yc = y - mu
    var = jnp.mean(yc * yc, axis=-1, keepdims=True)
    return yc * lax.rsqrt(var + LN_EPS) * g + b


def _rms_norm(y, g):
    return y * lax.rsqrt(jnp.mean(y * y, axis=-1, keepdims=True) + LN_EPS) * g


def _ada_kernel(c_ref, w_ref, b_ref, o_ref):
    cond = jax.nn.silu(c_ref[...]).astype(BF16)
    o_ref[...] = jnp.dot(cond, w_ref[...].astype(BF16), preferred_element_type=F32) + b_ref[...]


def _ada_mods(c, w_ada, b_ada):
    depth, d_model, n_out = w_ada.shape
    bsz = c.shape[0]
    rows = -(-bsz // SUBLANES) * SUBLANES
    c_pad = jnp.zeros((rows, d_model), F32).at[:bsz].set(c)
    tn = min(ADA_COLS, d_model)
    assert n_out % tn == 0
    out = pl.pallas_call(
        _ada_kernel,
        grid=(depth, n_out // tn),
        in_specs=[
            pl.BlockSpec((rows, d_model), lambda l, n: (0, 0)),
            pl.BlockSpec((None, d_model, tn), lambda l, n: (l, 0, n)),
            pl.BlockSpec((None, 1, tn), lambda l, n: (l, 0, n)),
        ],
        out_specs=pl.BlockSpec((None, rows, tn), lambda l, n: (l, 0, n)),
        out_shape=jax.ShapeDtypeStruct((depth, rows, n_out), F32),
        compiler_params=_cparams("parallel", "parallel"),
    )(c_pad, w_ada, b_ada.reshape(depth, 1, n_out))
    return out[:, :bsz].reshape(depth, bsz, 6, d_model)


def _in_proj_kernel(x_ref, mod_ref, w_ref, qkv1_ref, qkv4_ref, qkv16_ref, u_ref, h_sc, acc_sc,
                    *, tm, n_slabs):
    n = pl.program_id(2)

    @pl.when(n == 0)
    def _():
        sh = mod_ref[0:1, :]
        sc = mod_ref[1:2, :]
        h_sc[...] = (x_ref[...] * (1.0 + sc) + sh).astype(BF16)

    acc = jnp.dot(h_sc[...], w_ref[...], preferred_element_type=F32)

    @pl.when(n == 0)
    def _():
        u_ref[...] = acc

    @pl.when(n > 0)
    def _():
        accs = acc * jnp.where(n == 1, HEAD_DIM ** -0.5, 1.0).astype(F32)
        qkv1_ref[...] = accs.astype(BF16)
        for c in range(n_slabs):
            acc_sc[c] = accs[:, c * LANES:(c + 1) * LANES]
        for d, ref in ((4, qkv4_ref), (16, qkv16_ref)):
            for r in range(d):
                for c in range(n_slabs):
                    ref[r, :, c * LANES:(c + 1) * LANES] = (
                        acc_sc[c, pl.ds(r, tm // d, stride=d), :].astype(BF16))


def _in_proj(x, mod, w_in_bf16):
    bsz, seq, d_model = x.shape
    width = d_model // 2
    assert w_in_bf16.shape == (d_model, 4 * width) and width % LANES == 0
    tm = min(IN_PROJ_ROWS, seq)
    assert seq % tm == 0 and tm % (16 * 16) == 0
    n_slabs = width // LANES
    qkv_col = lambda n: jnp.maximum(n - 1, 0)
    return pl.pallas_call(
        functools.partial(_in_proj_kernel, tm=tm, n_slabs=n_slabs),
        grid=(bsz, seq // tm, 4),
        in_specs=[
            pl.BlockSpec((None, tm, d_model), lambda b, i, n: (b, i, 0)),
            pl.BlockSpec((None, 6, d_model), lambda b, i, n: (b, 0, 0)),
            pl.BlockSpec((d_model, width), lambda b, i, n: (0, (n + 3) % 4)),
        ],
        out_specs=[
            pl.BlockSpec((None, tm, width), lambda b, i, n: (b, i, qkv_col(n))),
            pl.BlockSpec((None, 4, tm // 4, width), lambda b, i, n: (b, 0, i, qkv_col(n))),
            pl.BlockSpec((None, 16, tm // 16, width), lambda b, i, n: (b, 0, i, qkv_col(n))),
            pl.BlockSpec((None, tm, width), lambda b, i, n: (b, i, 0)),
        ],
        out_shape=[
            jax.ShapeDtypeStruct((bsz, seq, 3 * width), BF16),
            jax.ShapeDtypeStruct((bsz, 4, seq // 4, 3 * width), BF16),
            jax.ShapeDtypeStruct((bsz, 16, seq // 16, 3 * width), BF16),
            jax.ShapeDtypeStruct((bsz, seq, width), F32),
        ],
        scratch_shapes=[
            pltpu.VMEM((tm, d_model), BF16),
            pltpu.VMEM((n_slabs, tm, LANES), F32),
        ],
        compiler_params=_cparams("parallel", "parallel", "arbitrary"),
    )(x, mod, w_in_bf16)


def _attn_bias_table(n_heads):
    slopes = jnp.exp2(-8.0 * jnp.arange(1, n_heads + 1, dtype=F32) / n_heads)
    qi = jnp.arange(ATTN_Q)[:, None]
    kc = jnp.arange(ATTN_K)[None, :]
    per_pattern = []
    for d in ATTN_DILATIONS:
        per_variant = []
        for v in range(3):
            rel = jnp.abs(kc - qi - ATTN_HALF * v)
            valid = rel <= ATTN_HALF
            dist = (rel * d).astype(F32)
            bias = jnp.where(valid[None], -slopes[:, None, None] * dist[None], NEG_BIG)
            per_variant.append(bias.reshape(n_heads // 2, 2, ATTN_Q, ATTN_K))
        per_pattern.append(jnp.stack(per_variant, axis=1))
    return jnp.stack(per_pattern, axis=1)


def _attn_kernel(q1_ref, k1_ref, v1_ref, q4_ref, k4_ref, v4_ref, q16_ref, k16_ref, v16_ref,
                 bias_ref, o_ref, acc_sc, m_sc, l_sc, *, seq):
    t = pl.program_id(2)
    lane = lax.broadcasted_iota(jnp.int32, (1, LANES), 1)
    head_a = lane < HEAD_DIM

    def block(p, load_q, load_k, load_v, qs, length):
        ks = jnp.clip(qs - ATTN_HALF, 0, length - ATTN_K)
        variant = (qs - ks) // ATTN_HALF
        ks = pl.multiple_of(ks, ATTN_HALF)
        q = load_q(qs)
        k = load_k(ks)
        v = load_v(ks)
        res = []
        for h in range(2):
            mask = head_a if h == 0 else jnp.logical_not(head_a)
            qh = jnp.where(mask, q, jnp.zeros_like(q))
            s = lax.dot_general(qh, k, (((1,), (1,)), ((), ())), preferred_element_type=F32)
            s = s + bias_ref[p, variant, h]
            m = jnp.max(s, axis=-1, keepdims=True)
            e = jnp.exp(s - m)
            l = jnp.sum(e, axis=-1, keepdims=True)
            o = jnp.dot(e.astype(BF16), v, preferred_element_type=F32)
            res.append((o, m, l))
        (o0, m0, l0), (o1, m1, l1) = res
        return (jnp.where(head_a, o0, o1), jnp.where(head_a, m0, m1), jnp.where(head_a, l0, l1))

    def body(j, carry):
        qs = pl.multiple_of(t * ATTN_TILE + j * ATTN_Q, ATTN_Q)
        acc, m, l = block(0,
                          lambda s: q1_ref[pl.ds(s, ATTN_Q), :],
                          lambda s: k1_ref[pl.ds(s, ATTN_K), :],
                          lambda s: v1_ref[pl.ds(s, ATTN_K), :],
                          qs, seq)
        rows = pl.ds(pl.multiple_of(j * ATTN_Q, ATTN_Q), ATTN_Q)
        acc_sc[0, rows, :] = acc
        m_sc[0, rows, :] = m
        l_sc[0, rows, :] = l
        r = j // 4
        jj = j % 4
        qs = pl.multiple_of(t * (ATTN_TILE // 4) + jj * ATTN_Q, ATTN_Q)
        acc, m, l = block(1,
                          lambda s: q4_ref[r, pl.ds(s, ATTN_Q), :],
                          lambda s: k4_ref[r, pl.ds(s, ATTN_K), :],
                          lambda s: v4_ref[r, pl.ds(s, ATTN_K), :],
                          qs, seq // 4)
        rows = pl.ds(r + 4 * ATTN_Q * jj, ATTN_Q, stride=4)
        acc_sc[1, rows, :] = acc
        m_sc[1, rows, :] = m
        l_sc[1, rows, :] = l
        qs = pl.multiple_of(t * ATTN_Q, ATTN_Q)
        acc, m, l = block(2,
                          lambda s: q16_ref[j, pl.ds(s, ATTN_Q), :],
                          lambda s: k16_ref[j, pl.ds(s, ATTN_K), :],
                          lambda s: v16_ref[j, pl.ds(s, ATTN_K), :],
                          qs, seq // 16)
        rows = pl.ds(j, ATTN_Q, stride=16)
        acc_sc[2, rows, :] = acc
        m_sc[2, rows, :] = m
        l_sc[2, rows, :] = l
        return carry

    lax.fori_loop(0, 16, body, 0)

    def combine(i, carry):
        rows = pl.ds(pl.multiple_of(i * 256, 256), 256)
        ms = [m_sc[p, rows, :] for p in range(3)]
        m_all = jnp.maximum(jnp.maximum(ms[0], ms[1]), ms[2])
        num = jnp.zeros((256, LANES), F32)
        den = jnp.zeros((256, LANES), F32)
        for p in range(3):
            w = jnp.exp(ms[p] - m_all)
            num = num + w * acc_sc[p, rows, :]
            den = den + w * l_sc[p, rows, :]
        o_ref[rows, :] = num / den
        return carry

    lax.fori_loop(0, ATTN_TILE // 256, combine, 0)


def _attention(qkv1, qkv4, qkv16, bias_table):
    bsz, seq, w3 = qkv1.shape
    width = w3 // 3
    n_pairs = width // LANES
    assert seq % ATTN_TILE == 0 and seq // 16 >= ATTN_K
    col = lambda which: (lambda b, hp, t: (b, 0, which * n_pairs + hp))
    col4 = lambda which: (lambda b, hp, t: (b, 0, 0, which * n_pairs + hp))
    specs = []
    for d, cmap in ((1, col), (4, col4), (16, col4)):
        for which in range(3):
            if d == 1:
                specs.append(pl.BlockSpec((None, seq, LANES), cmap(which)))
            else:
                specs.append(pl.BlockSpec((None, d, seq // d, LANES), cmap(which)))
    specs.append(pl.BlockSpec((None, 3, 3, 2, ATTN_Q, ATTN_K), lambda b, hp, t: (hp, 0, 0, 0, 0, 0)))
    return pl.pallas_call(
        functools.partial(_attn_kernel, seq=seq),
        grid=(bsz, n_pairs, seq // ATTN_TILE),
        in_specs=specs,
        out_specs=pl.BlockSpec((None, ATTN_TILE, LANES), lambda b, hp, t: (b, t, hp)),
        out_shape=jax.ShapeDtypeStruct((bsz, seq, width), F32),
        scratch_shapes=[pltpu.VMEM((3, ATTN_TILE, LANES), F32)] * 3,
        compiler_params=_cparams("parallel", "parallel", "arbitrary"),
    )(qkv1, qkv1, qkv1, qkv4, qkv4, qkv4, qkv16, qkv16, qkv16, bias_table)


def _s5_prepare(lam_re, lam_im, log_step, b_re, b_im, c_re, c_im):
    depth, _, n_groups, n_state = lam_re.shape
    gps = S5_GROUPS_PER_STEP
    n_gb = n_groups // gps
    width = gps * n_state
    step = jnp.exp(log_step)[..., None]
    mag = jnp.exp(lam_re * step)
    a_re, a_im = mag * jnp.cos(lam_im * step), mag * jnp.sin(lam_im * step)
    den = lam_re * lam_re + lam_im * lam_im
    coef_re = ((a_re - 1.0) * lam_re + a_im * lam_im) / den
    coef_im = (a_im * lam_re - (a_re - 1.0) * lam_im) / den
    bb_re = coef_re[..., None] * b_re - coef_im[..., None] * b_im
    bb_im = coef_re[..., None] * b_im + coef_im[..., None] * b_re
    eye = jnp.eye(gps, dtype=F32)

    def block_diag_in(bb):
        t = bb.reshape(depth, 2, n_gb, gps, n_state, SSM_GROUP).transpose(0, 1, 2, 3, 5, 4)
        full = t[:, :, :, :, :, None, :] * eye[None, None, None, :, None, :, None]
        return full.reshape(depth, 2, n_gb, gps * SSM_GROUP, width)

    def block_diag_out(cc):
        t = cc.reshape(depth, 2, n_gb, gps, SSM_GROUP, n_state).transpose(0, 1, 2, 3, 5, 4)
        full = t[:, :, :, :, :, None, :] * eye[None, None, None, :, None, :, None]
        return full.reshape(depth, 2, n_gb, width, gps * SSM_GROUP)

    b_mat = jnp.concatenate([block_diag_in(bb_re), block_diag_in(bb_im)], axis=-1).astype(BF16)
    c_mat = jnp.concatenate([block_diag_out(c_re), -block_diag_out(c_im)], axis=-2).astype(BF16)

    def lanes(t):
        return t.reshape(depth, 2, n_gb, width)

    ar, ai = lanes(a_re), lanes(a_im)

    def cmul(x, y):
        return (x[0] * y[0] - x[1] * y[1], x[0] * y[1] + x[1] * y[0])

    reps = (S5_SUB,) + (1,) * ar.ndim
    pr, pi = lax.associative_scan(cmul, (jnp.tile(ar[None], reps), jnp.tile(ai[None], reps)), axis=0)
    pow_tab = jnp.stack([pr, pi], axis=0).transpose(2, 3, 4, 0, 1, 5)
    pow_tab = jnp.broadcast_to(pow_tab[..., None, :], pow_tab.shape[:-1] + (SUBLANES, width))
    a64 = (pr[-1], pi[-1])
    a128 = cmul(a64, a64)
    a256 = cmul(a128, a128)

    def bc(t):
        return jnp.broadcast_to(t[..., None, :], t.shape[:-1] + (SUBLANES, width))

    a_tab = jnp.stack([jnp.stack([bc(x[0]), bc(x[1])], axis=3)
                       for x in ((ar, ai), a64, a128, a256)], axis=3)
    return b_mat, c_mat, a_tab, pow_tab


def _s5_kernel(u_ref, bm_ref, cm_ref, a_ref, pow_ref, dsk_ref, y_ref,
               lhs_sc, x_sc, ysc, carry_sc, cfix_sc, *, seq, width):
    n_col = width // LANES
    n_blk = seq // S5_BLOCK
    row = lax.broadcasted_iota(jnp.int32, (SUBLANES, LANES), 0)
    dsk = dsk_ref[...]

    def col(c):
        return slice(c * LANES, (c + 1) * LANES)

    def run_direction(d):
        fwd = d == 0
        carry_sc[...] = jnp.zeros_like(carry_sc)
        a_cols = [(a_ref[d, 0, 0, :, col(c)], a_ref[d, 0, 1, :, col(c)]) for c in range(n_col)]

        def block_body(blk, carry):
            tb = blk if fwd else n_blk - 1 - blk
            rows0 = pl.multiple_of(tb * S5_BLOCK, S5_BLOCK)
            for s in range(SUBLANES):
                lhs_sc[pl.ds(s, S5_SUB, stride=SUBLANES), :] = u_ref[pl.ds(rows0 + s * S5_SUB, S5_SUB), :]
            bu = jnp.dot(lhs_sc[...].astype(BF16), bm_ref[d], preferred_element_type=F32)
            for c in range(2 * n_col):
                x_sc[c] = bu[:, col(c)]

            def scan_body(jj, state):
                j = jj if fwd else S5_SUB - 1 - jj
                r8 = pl.ds(pl.multiple_of(j * SUBLANES, SUBLANES), SUBLANES)
                new = []
                for c in range(n_col):
                    xr, xi = state[2 * c], state[2 * c + 1]
                    ar, ai = a_cols[c]
                    nr = ar * xr - ai * xi + x_sc[c, r8, :]
                    ni = ar * xi + ai * xr + x_sc[n_col + c, r8, :]
                    x_sc[c, r8, :] = nr
                    x_sc[n_col + c, r8, :] = ni
                    new += [nr, ni]
                return tuple(new)

            zero = jnp.zeros((SUBLANES, LANES), F32)
            ends = lax.fori_loop(0, S5_SUB, scan_body, (zero,) * (2 * n_col), unroll=4)

            for c in range(n_col):
                er, ei = ends[2 * c], ends[2 * c + 1]
                if fwd:
                    cin_r = carry_sc[0, SUBLANES - 1:SUBLANES, col(c)]
                    cin_i = carry_sc[1, SUBLANES - 1:SUBLANES, col(c)]
                    cr = jnp.where(row == 0, cin_r, pltpu.roll(er, 1, 0))
                    ci = jnp.where(row == 0, cin_i, pltpu.roll(ei, 1, 0))
                else:
                    cin_r = carry_sc[0, 0:1, col(c)]
                    cin_i = carry_sc[1, 0:1, col(c)]
                    cr = jnp.where(row == SUBLANES - 1, cin_r, pltpu.roll(er, SUBLANES - 1, 0))
                    ci = jnp.where(row == SUBLANES - 1, cin_i, pltpu.roll(ei, SUBLANES - 1, 0))
                for k, pw in ((1, 1), (2, 2), (4, 3)):
                    pr, pi = a_ref[d, pw, 0, :, col(c)], a_ref[d, pw, 1, :, col(c)]
                    if fwd:
                        sr = jnp.where(row >= k, pltpu.roll(cr, k, 0), 0.0)
                        si = jnp.where(row >= k, pltpu.roll(ci, k, 0), 0.0)
                    else:
                        sr = jnp.where(row < SUBLANES - k, pltpu.roll(cr, SUBLANES - k, 0), 0.0)
                        si = jnp.where(row < SUBLANES - k, pltpu.roll(ci, SUBLANES - k, 0), 0.0)
                    cr, ci = cr + pr * sr - pi * si, ci + pr * si + pi * sr
                pr, pi = a_ref[d, 1, 0, :, col(c)], a_ref[d, 1, 1, :, col(c)]
                carry_sc[0, :, col(c)] = pr * cr - pi * ci + er
                carry_sc[1, :, col(c)] = pr * ci + pi * cr + ei
                cfix_sc[0, :, col(c)] = cr
                cfix_sc[1, :, col(c)] = ci

            def fix_body(j, carry2):
                r8 = pl.ds(pl.multiple_of(j * SUBLANES, SUBLANES), SUBLANES)
                pj = j if fwd else S5_SUB - 1 - j
                for c in range(n_col):
                    pr = pow_ref[d, 0, pj, :, col(c)]
                    pi = pow_ref[d, 1, pj, :, col(c)]
                    cr, ci = cfix_sc[0, :, col(c)], cfix_sc[1, :, col(c)]
                    x_sc[c, r8, :] = x_sc[c, r8, :] + (pr * cr - pi * ci)
                    x_sc[n_col + c, r8, :] = x_sc[n_col + c, r8, :] + (pr * ci + pi * cr)
                return carry2

            lax.fori_loop(0, S5_SUB, fix_body, 0, unroll=4)

            xcat = jnp.concatenate([x_sc[c] for c in range(2 * n_col)], axis=-1).astype(BF16)
            ysc[...] = jnp.dot(xcat, cm_ref[d], preferred_element_type=F32)
            for s in range(SUBLANES):
                rows = pl.ds(rows0 + s * S5_SUB, S5_SUB)
                ys = ysc[pl.ds(s, S5_SUB, stride=SUBLANES), :]
                if fwd:
                    y_ref[rows, :] = ys
                else:
                    y_ref[rows, :] = jax.nn.gelu(y_ref[rows, :] + ys + dsk * u_ref[rows, :])
            return carry

        lax.fori_loop(0, n_blk, block_body, 0)

    run_direction(0)
    run_direction(1)


def _s5(u, b_mat, c_mat, a_tab, pow_tab, d_skip):
    bsz, seq, w = u.shape
    n_gb = w // LANES
    width = b_mat.shape[-1] // 2
    assert seq % S5_BLOCK == 0 and width % LANES == 0
    n_col = width // LANES
    return pl.pallas_call(
        functools.partial(_s5_kernel, seq=seq, width=width),
        grid=(bsz, n_gb),
        in_specs=[
            pl.BlockSpec((None, seq, LANES), lambda b, g: (b, 0, g)),
            pl.BlockSpec((2, None, LANES, 2 * width), lambda b, g: (0, g, 0, 0)),
            pl.BlockSpec((2, None, 2 * width, LANES), lambda b, g: (0, g, 0, 0)),
            pl.BlockSpec((2, None, 4, 2, SUBLANES, width), lambda b, g: (0, g, 0, 0, 0, 0)),
            pl.BlockSpec((2, None, 2, S5_SUB, SUBLANES, width), lambda b, g: (0, g, 0, 0, 0, 0)),
            pl.BlockSpec((1, LANES), lambda b, g: (0, g)),
        ],
        out_specs=pl.BlockSpec((None, seq, LANES), lambda b, g: (b, 0, g)),
        out_shape=jax.ShapeDtypeStruct((bsz, seq, w), F32),
        scratch_shapes=[
            pltpu.VMEM((S5_BLOCK, LANES), F32),
            pltpu.VMEM((2 * n_col, S5_BLOCK, LANES), F32),
            pltpu.VMEM((S5_BLOCK, LANES), F32),
            pltpu.VMEM((2, SUBLANES, width), F32),
            pltpu.VMEM((2, SUBLANES, width), F32),
        ],
        compiler_params=_cparams("parallel", "parallel"),
    )(u, b_mat, c_mat, a_tab, pow_tab, d_skip.reshape(1, w))


def _mix_out_kernel(attn_ref, g_ref, x_ref, mod_ref, wglu_ref, bglu_ref, gattn_ref, gssm_ref,
                    wout_ref, lng_ref, lnb_ref, o_ref, *, alpha, width):
    g = g_ref[...]
    z = jnp.dot(g.astype(BF16), wglu_ref[...], preferred_element_type=F32) + bglu_ref[...]
    ssm = g * jax.nn.sigmoid(z)
    ra = _rms_norm(attn_ref[...], gattn_ref[...]).astype(BF16)
    rs = _rms_norm(ssm, gssm_ref[...]).astype(BF16)
    mix = (jnp.dot(ra, wout_ref[0:width, :], preferred_element_type=F32)
           + jnp.dot(rs, wout_ref[width:2 * width, :], preferred_element_type=F32))
    gate = mod_ref[2:3, :]
    y = alpha * x_ref[...] + (1.0 + gate) * mix
    o_ref[...] = _layer_norm(y, lng_ref[...], lnb_ref[...])


def _mix_out(attn, g, x, mod, w_glu, b_glu, g_attn, g_ssm, w_out, ln_g, ln_b, alpha):
    bsz, seq, d_model = x.shape
    width = d_model // 2
    tm = min(MIX_ROWS, seq)
    assert seq % tm == 0
    row = lambda b, i: (b, i, 0)
    fixed = lambda b, i: (0, 0)
    return pl.pallas_call(
        functools.partial(_mix_out_kernel, alpha=alpha, width=width),
        grid=(bsz, seq // tm),
        in_specs=[
            pl.BlockSpec((None, tm, width), row),
            pl.BlockSpec((None, tm, width), row),
            pl.BlockSpec((None, tm, d_model), row),
            pl.BlockSpec((None, 6, d_model), lambda b, i: (b, 0, 0)),
            pl.BlockSpec((width, width), fixed),
            pl.BlockSpec((1, width), fixed),
            pl.BlockSpec((1, width), fixed),
            pl.BlockSpec((1, width), fixed),
            pl.BlockSpec((d_model, d_model), fixed),
            pl.BlockSpec((1, d_model), fixed),
            pl.BlockSpec((1, d_model), fixed),
        ],
        out_specs=pl.BlockSpec((None, tm, d_model), row),
        out_shape=jax.ShapeDtypeStruct((bsz, seq, d_model), F32),
        compiler_params=_cparams("parallel", "parallel"),
    )(attn, g, x, mod, w_glu, b_glu.reshape(1, width), g_attn.reshape(1, width),
      g_ssm.reshape(1, width), w_out, ln_g.reshape(1, d_model), ln_b.reshape(1, d_model))


def _mlp_kernel(x_ref, mod_ref, w1_ref, w2_ref, lng_ref, lnb_ref, o_ref, h_sc, acc_sc, *, alpha):
    kf = pl.program_id(2)

    @pl.when(kf == 0)
    def _():
        sh = mod_ref[3:4, :]
        sc = mod_ref[4:5, :]
        h_sc[...] = (x_ref[...] * (1.0 + sc) + sh).astype(BF16)
        acc_sc[...] = jnp.zeros_like(acc_sc)

    a = jnp.maximum(jnp.dot(h_sc[...], w1_ref[...], preferred_element_type=F32), 0.0)
    acc_sc[...] += jnp.dot((a * a).astype(BF16), w2_ref[...], preferred_element_type=F32)

    @pl.when(kf == pl.num_programs(2) - 1)
    def _():
        gate = mod_ref[5:6, :]
        y = alpha * x_ref[...] + (1.0 + gate) * acc_sc[...]
        o_ref[...] = _layer_norm(y, lng_ref[...], lnb_ref[...])


def _mlp(x, mod, w1, w2, ln_g, ln_b, alpha):
    bsz, seq, d_model = x.shape
    d_ff = w1.shape[1]
    tm = min(MLP_ROWS, seq)
    tf = min(MLP_FF_CHUNK, d_ff)
    assert seq % tm == 0 and d_ff % tf == 0
    return pl.pallas_call(
        functools.partial(_mlp_kernel, alpha=alpha),
        grid=(bsz, seq // tm, d_ff // tf),
        in_specs=[
            pl.BlockSpec((None, tm, d_model), lambda b, i, k: (b, i, 0)),
            pl.BlockSpec((None, 6, d_model), lambda b, i, k: (b, 0, 0)),
            pl.BlockSpec((d_model, tf), lambda b, i, k: (0, k)),
            pl.BlockSpec((tf, d_model), lambda b, i, k: (k, 0)),
            pl.BlockSpec((1, d_model), lambda b, i, k: (0, 0)),
            pl.BlockSpec((1, d_model), lambda b, i, k: (0, 0)),
        ],
        out_specs=pl.BlockSpec((None, tm, d_model), lambda b, i, k: (b, i, 0)),
        out_shape=jax.ShapeDtypeStruct((bsz, seq, d_model), F32),
        scratch_shapes=[
            pltpu.VMEM((tm, d_model), BF16),
            pltpu.VMEM((tm, d_model), F32),
        ],
        compiler_params=_cparams("parallel", "parallel", "arbitrary"),
    )(x, mod, w1, w2, ln_g.reshape(1, d_model), ln_b.reshape(1, d_model))


def kernel(x, c, w_ada, b_ada, w_in, ssm_lam_re, ssm_lam_im, ssm_log_step, ssm_b_re, ssm_b_im,
           ssm_c_re, ssm_c_im, ssm_d, w_glu, b_glu, g_attn, g_ssm, w_out, ln1_g, ln1_b,
           w_mlp1, w_mlp2, ln2_g, ln2_b):
    depth = w_ada.shape[0]
    d_model = x.shape[-1]
    n_heads = (d_model // 2) // HEAD_DIM
    alpha = float((2 * depth) ** 0.25)

    mods = _ada_mods(c, w_ada, b_ada)
    b_mat, c_mat, a_tab, pow_tab = _s5_prepare(ssm_lam_re, ssm_lam_im, ssm_log_step,
                                               ssm_b_re, ssm_b_im, ssm_c_re, ssm_c_im)
    bias_table = _attn_bias_table(n_heads)
    w_in_h, w_glu_h, w_out_h = w_in.astype(BF16), w_glu.astype(BF16), w_out.astype(BF16)
    w1_h, w2_h = w_mlp1.astype(BF16), w_mlp2.astype(BF16)

    for l in range(depth):
        qkv1, qkv4, qkv16, u = _in_proj(x, mods[l], w_in_h[l])
        attn = _attention(qkv1, qkv4, qkv16, bias_table)
        g = _s5(u, b_mat[l], c_mat[l], a_tab[l], pow_tab[l], ssm_d[l])
        x = _mix_out(attn, g, x, mods[l], w_glu_h[l], b_glu[l], g_attn[l], g_ssm[l], w_out_h[l],
                     ln1_g[l], ln1_b[l], alpha)
        x = _mlp(x, mods[l], w1_h[l], w2_h[l], ln2_g[l], ln2_b[l], alpha)
    return x
```

```python
import functools

import jax
import jax.numpy as jnp
from jax import lax
from jax.experimental import pallas as pl
from jax.experimental.pallas import tpu as pltpu

F32 = jnp.float32
BF16 = jnp.bfloat16

LANES = 128
SUBLANES = 8
MXU_COLS = 256
VMEM_LIMIT_BYTES = 56 * 1024 * 1024

HEAD_DIM = 64
ATTN_DILATIONS = (1, 4, 16)
ATTN_HALF = 64
ATTN_Q = 128
ATTN_K = ATTN_Q + 2 * ATTN_HALF
ATTN_TILE = ATTN_Q * max(ATTN_DILATIONS)
SSM_GROUP = 16
S5_GROUPS_PER_STEP = LANES // SSM_GROUP
S5_SUB = 32
S5_BLOCK = SUBLANES * S5_SUB
LN_EPS = 1e-5
NEG_BIG = -1e30

IN_PROJ_ROWS = 512
MIX_ROWS = 512
MIX_SUB_ROWS = 256
MLP_ROWS = 512
MLP_FF_CHUNK = 512
ADA_COLS = 1024


def _cparams(*sem):
    return pltpu.CompilerParams(dimension_semantics=sem, vmem_limit_bytes=VMEM_LIMIT_BYTES)


def _layer_norm(y, g, b):
    mu = jnp.mean(y, axis=-1, keepdims=True)
    yc = y - mu
    var = jnp.mean(yc * yc, axis=-1, keepdims=True)
    return yc * lax.rsqrt(var + LN_EPS) * g + b


def _rms_norm(y, g):
    return y * lax.rsqrt(jnp.mean(y * y, axis=-1, keepdims=True) + LN_EPS) * g


def _ada_kernel(c_ref, w_ref, b_ref, o_ref):
    cond = jax.nn.silu(c_ref[...]).astype(BF16)
    o_ref[...] = jnp.dot(cond, w_ref[...].astype(BF16), preferred_element_type=F32) + b_ref[...]


def _ada_mods(c, w_ada, b_ada):
    depth, d_model, n_out = w_ada.shape
    bsz = c.shape[0]
    rows = -(-bsz // SUBLANES) * SUBLANES
    c_pad = jnp.zeros((rows, d_model), F32).at[:bsz].set(c)
    tn = min(ADA_COLS, d_model)
    assert n_out % tn == 0
    out = pl.pallas_call(
        _ada_kernel,
        grid=(depth, n_out // tn),
        in_specs=[
            pl.BlockSpec((rows, d_model), lambda l, n: (0, 0)),
            pl.BlockSpec((None, d_model, tn), lambda l, n: (l, 0, n)),
            pl.BlockSpec((None, 1, tn), lambda l, n: (l, 0, n)),
        ],
        out_specs=pl.BlockSpec((None, rows, tn), lambda l, n: (l, 0, n)),
        out_shape=jax.ShapeDtypeStruct((depth, rows, n_out), F32),
        compiler_params=_cparams("parallel", "parallel"),
    )(c_pad, w_ada, b_ada.reshape(depth, 1, n_out))
    return out[:, :bsz].reshape(depth, bsz, 6, d_model)


def _in_proj_kernel(x_ref, mod_ref, w_ref, qkv1_ref, qkv4_ref, qkv16_ref, u_ref, h_sc, acc_sc,
                    acc4_sc, *, tm, n_slabs):
    n = pl.program_id(2)

    @pl.when(n == 0)
    def _():
        sh = mod_ref[0:1, :]
        sc = mod_ref[1:2, :]
        h_sc[...] = (x_ref[...] * (1.0 + sc) + sh).astype(BF16)

    @pl.when(n == 0)
    def _():
        u_ref[...] = jnp.dot(h_sc[...], w_ref[...], preferred_element_type=F32)

    @pl.when(n > 0)
    def _():
        scale = jnp.where(n == 1, HEAD_DIM ** -0.5, 1.0).astype(F32)
        per = min(MXU_COLS // LANES, n_slabs)
        for c0 in range(0, n_slabs, per):
            wide = slice(c0 * LANES, (c0 + per) * LANES)
            acc = jnp.dot(h_sc[...], w_ref[:, wide], preferred_element_type=F32) * scale
            qkv1_ref[:, wide] = acc.astype(BF16)
            for c in range(c0, c0 + per):
                cols = slice(c * LANES, (c + 1) * LANES)
                acc_sc[c] = acc[:, (c - c0) * LANES:(c - c0 + 1) * LANES]
                for r4 in range(4):
                    cls = acc_sc[c, pl.ds(r4, tm // 4, stride=4), :]
                    qkv4_ref[r4, :, cols] = cls.astype(BF16)
                    acc4_sc[c * 4 + r4] = cls
                    for rr in range(4):
                        qkv16_ref[4 * rr + r4, :, cols] = (
                            acc4_sc[c * 4 + r4, pl.ds(rr, tm // 16, stride=4), :].astype(BF16))


def _in_proj(x, mod, w_in_bf16):
    bsz, seq, d_model = x.shape
    width = d_model // 2
    assert w_in_bf16.shape == (d_model, 4 * width) and width % LANES == 0
    tm = min(IN_PROJ_ROWS, seq)
    assert seq % tm == 0 and tm % (16 * 16) == 0
    n_slabs = width // LANES
    qkv_col = lambda n: jnp.maximum(n - 1, 0)
    return pl.pallas_call(
        functools.partial(_in_proj_kernel, tm=tm, n_slabs=n_slabs),
        grid=(bsz, seq // tm, 4),
        in_specs=[
            pl.BlockSpec((None, tm, d_model), lambda b, i, n: (b, i, 0)),
            pl.BlockSpec((None, 6, d_model), lambda b, i, n: (b, 0, 0)),
            pl.BlockSpec((d_model, width), lambda b, i, n: (0, (n + 3) % 4)),
        ],
        out_specs=[
            pl.BlockSpec((None, tm, width), lambda b, i, n: (b, i, qkv_col(n))),
            pl.BlockSpec((None, 4, tm // 4, width), lambda b, i, n: (b, 0, i, qkv_col(n))),
            pl.BlockSpec((None, 16, tm // 16, width), lambda b, i, n: (b, 0, i, qkv_col(n))),
            pl.BlockSpec((None, tm, width), lambda b, i, n: (b, i, 0)),
        ],
        out_shape=[
            jax.ShapeDtypeStruct((bsz, seq, 3 * width), BF16),
            jax.ShapeDtypeStruct((bsz, 4, seq // 4, 3 * width), BF16),
            jax.ShapeDtypeStruct((bsz, 16, seq // 16, 3 * width), BF16),
            jax.ShapeDtypeStruct((bsz, seq, width), F32),
        ],
        scratch_shapes=[
            pltpu.VMEM((tm, d_model), BF16),
            pltpu.VMEM((n_slabs, tm, LANES), F32),
            pltpu.VMEM((n_slabs * 4, tm // 4, LANES), F32),
        ],
        compiler_params=_cparams("parallel", "parallel", "arbitrary"),
    )(x, mod, w_in_bf16)


def _attn_bias_table(n_heads):
    slopes = jnp.exp2(-8.0 * jnp.arange(1, n_heads + 1, dtype=F32) / n_heads)
    qi = jnp.arange(ATTN_Q)[:, None]
    kc = jnp.arange(ATTN_K)[None, :]
    per_pattern = []
    for d in ATTN_DILATIONS:
        per_variant = []
        for v in range(3):
            rel = jnp.abs(kc - qi - ATTN_HALF * v)
            valid = rel <= ATTN_HALF
            dist = (rel * d).astype(F32)
            bias = jnp.where(valid[None], -slopes[:, None, None] * dist[None], NEG_BIG)
            per_variant.append(bias.reshape(n_heads // 2, 2, ATTN_Q, ATTN_K))
        per_pattern.append(jnp.stack(per_variant, axis=1))
    return jnp.stack(per_pattern, axis=1)


def _attn_kernel(q1_ref, k1_ref, v1_ref, q4_ref, k4_ref, v4_ref, q16_ref, k16_ref, v16_ref,
                 bias_ref, o_ref, acc_sc, m_sc, l_sc, *, seq):
    t = pl.program_id(2)
    lane = lax.broadcasted_iota(jnp.int32, (1, LANES), 1)
    head_a = lane < HEAD_DIM

    def block(p, load_q, load_k, load_v, qs, length):
        ks = jnp.clip(qs - ATTN_HALF, 0, length - ATTN_K)
        variant = (qs - ks) // ATTN_HALF
        ks = pl.multiple_of(ks, ATTN_HALF)
        q = load_q(qs)
        k = load_k(ks)
        v = load_v(ks)
        res = []
        for h in range(2):
            mask = head_a if h == 0 else jnp.logical_not(head_a)
            qh = jnp.where(mask, q, jnp.zeros_like(q))
            s = lax.dot_general(qh, k, (((1,), (1,)), ((), ())), preferred_element_type=F32)
            s = s + bias_ref[p, variant, h]
            m = jnp.max(s, axis=-1, keepdims=True)
            e = jnp.exp(s - m)
            l = jnp.sum(e, axis=-1, keepdims=True)
            o = jnp.dot(e.astype(BF16), v, preferred_element_type=F32)
            res.append((o, m, l))
        (o0, m0, l0), (o1, m1, l1) = res
        return (jnp.where(head_a, o0, o1), jnp.where(head_a, m0, m1), jnp.where(head_a, l0, l1))

    def body(j):
        qs = pl.multiple_of(t * ATTN_TILE + j * ATTN_Q, ATTN_Q)
        acc, m, l = block(0,
                          lambda s: q1_ref[pl.ds(s, ATTN_Q), :],
                          lambda s: k1_ref[pl.ds(s, ATTN_K), :],
                          lambda s: v1_ref[pl.ds(s, ATTN_K), :],
                          qs, seq)
        rows = pl.ds(j * ATTN_Q, ATTN_Q)
        acc_sc[0, rows, :] = acc
        m_sc[0, rows, :] = m
        l_sc[0, rows, :] = l
        r = j // 4
        jj = j % 4
        qs = pl.multiple_of(t * (ATTN_TILE // 4) + jj * ATTN_Q, ATTN_Q)
        acc, m, l = block(1,
                          lambda s: q4_ref[r, pl.ds(s, ATTN_Q), :],
                          lambda s: k4_ref[r, pl.ds(s, ATTN_K), :],
                          lambda s: v4_ref[r, pl.ds(s, ATTN_K), :],
                          qs, seq // 4)
        rows = pl.ds(r + 4 * ATTN_Q * jj, ATTN_Q, stride=4)
        acc_sc[1, rows, :] = acc
        m_sc[1, rows, :] = m
        l_sc[1, rows, :] = l
        qs = pl.multiple_of(t * ATTN_Q, ATTN_Q)
        acc, m, l = block(2,
                          lambda s: q16_ref[j, pl.ds(s, ATTN_Q), :],
                          lambda s: k16_ref[j, pl.ds(s, ATTN_K), :],
                          lambda s: v16_ref[j, pl.ds(s, ATTN_K), :],
                          qs, seq // 16)
        rows = pl.ds(j, ATTN_Q, stride=16)
        acc_sc[2, rows, :] = acc
        m_sc[2, rows, :] = m
        l_sc[2, rows, :] = l

    for j in range(ATTN_TILE // ATTN_Q):
        body(j)

    def combine(i, carry):
        rows = pl.ds(pl.multiple_of(i * 256, 256), 256)
        ms = [m_sc[p, rows, :] for p in range(3)]
        m_all = jnp.maximum(jnp.maximum(ms[0], ms[1]), ms[2])
        num = jnp.zeros((256, LANES), F32)
        den = jnp.zeros((256, LANES), F32)
        for p in range(3):
            w = jnp.exp(ms[p] - m_all)
            num = num + w * acc_sc[p, rows, :]
            den = den + w * l_sc[p, rows, :]
        o_ref[rows, :] = num / den
        return carry

    lax.fori_loop(0, ATTN_TILE // 256, combine, 0)


def _attention(qkv1, qkv4, qkv16, bias_table):
    bsz, seq, w3 = qkv1.shape
    width = w3 // 3
    n_pairs = width // LANES
    assert seq % ATTN_TILE == 0 and seq // 16 >= ATTN_K
    col = lambda which: (lambda b, hp, t: (b, 0, which * n_pairs + hp))
    col4 = lambda which: (lambda b, hp, t: (b, 0, 0, which * n_pairs + hp))
    specs = []
    for d, cmap in ((1, col), (4, col4), (16, col4)):
        for which in range(3):
            if d == 1:
                specs.append(pl.BlockSpec((None, seq, LANES), cmap(which)))
            else:
                specs.append(pl.BlockSpec((None, d, seq // d, LANES), cmap(which)))
    specs.append(pl.BlockSpec((None, 3, 3, 2, ATTN_Q, ATTN_K), lambda b, hp, t: (hp, 0, 0, 0, 0, 0)))
    return pl.pallas_call(
        functools.partial(_attn_kernel, seq=seq),
        grid=(bsz, n_pairs, seq // ATTN_TILE),
        in_specs=specs,
        out_specs=pl.BlockSpec((None, ATTN_TILE, LANES), lambda b, hp, t: (b, t, hp)),
        out_shape=jax.ShapeDtypeStruct((bsz, seq, width), F32),
        scratch_shapes=[pltpu.VMEM((3, ATTN_TILE, LANES), F32)] * 3,
        compiler_params=_cparams("parallel", "parallel", "arbitrary"),
    )(qkv1, qkv1, qkv1, qkv4, qkv4, qkv4, qkv16, qkv16, qkv16, bias_table)


def _s5_prepare(lam_re, lam_im, log_step, b_re, b_im, c_re, c_im):
    depth, _, n_groups, n_state = lam_re.shape
    gps = S5_GROUPS_PER_STEP
    n_gb = n_groups // gps
    width = gps * n_state
    step = jnp.exp(log_step)[..., None]
    mag = jnp.exp(lam_re * step)
    a_re, a_im = mag * jnp.cos(lam_im * step), mag * jnp.sin(lam_im * step)
    den = lam_re * lam_re + lam_im * lam_im
    coef_re = ((a_re - 1.0) * lam_re + a_im * lam_im) / den
    coef_im = (a_im * lam_re - (a_re - 1.0) * lam_im) / den
    bb_re = coef_re[..., None] * b_re - coef_im[..., None] * b_im
    bb_im = coef_re[..., None] * b_im + coef_im[..., None] * b_re
    eye = jnp.eye(gps, dtype=F32)

    def block_diag_in(bb):
        t = bb.reshape(depth, 2, n_gb, gps, n_state, SSM_GROUP).transpose(0, 1, 2, 3, 5, 4)
        full = t[:, :, :, :, :, None, :] * eye[None, None, None, :, None, :, None]
        return full.reshape(depth, 2, n_gb, gps * SSM_GROUP, width)

    def block_diag_out(cc):
        t = cc.reshape(depth, 2, n_gb, gps, SSM_GROUP, n_state).transpose(0, 1, 2, 3, 5, 4)
        full = t[:, :, :, :, :, None, :] * eye[None, None, None, :, None, :, None]
        return full.reshape(depth, 2, n_gb, width, gps * SSM_GROUP)

    b_mat = jnp.concatenate([block_diag_in(bb_re), block_diag_in(bb_im)], axis=-1).astype(BF16)
    c_mat = jnp.concatenate([block_diag_out(c_re), -block_diag_out(c_im)], axis=-2).astype(BF16)

    def lanes(t):
        return t.reshape(depth, 2, n_gb, width)

    ar, ai = lanes(a_re), lanes(a_im)

    def cmul(x, y):
        return (x[0] * y[0] - x[1] * y[1], x[0] * y[1] + x[1] * y[0])

    reps = (S5_SUB,) + (1,) * ar.ndim
    pr, pi = lax.associative_scan(cmul, (jnp.tile(ar[None], reps), jnp.tile(ai[None], reps)), axis=0)
    pow_tab = jnp.stack([pr, pi], axis=0).transpose(2, 3, 4, 0, 1, 5)
    pow_tab = jnp.broadcast_to(pow_tab[..., None, :], pow_tab.shape[:-1] + (SUBLANES, width))
    a_sub = (pr[-1], pi[-1])
    a_sub2 = cmul(a_sub, a_sub)
    a_sub4 = cmul(a_sub2, a_sub2)

    def bc(t):
        return jnp.broadcast_to(t[..., None, :], t.shape[:-1] + (SUBLANES, width))

    a_tab = jnp.stack([jnp.stack([bc(x[0]), bc(x[1])], axis=3)
                       for x in ((ar, ai), a_sub, a_sub2, a_sub4)], axis=3)
    return b_mat, c_mat, a_tab, pow_tab


def _s5_kernel(u_ref, bm_ref, cm_ref, a_ref, pow_ref, dsk_ref, y_ref,
               lhs_sc, x_sc, ysc, carry_sc, *, seq, width):
    n_col = width // LANES
    n_blk = seq // S5_BLOCK
    row = lax.broadcasted_iota(jnp.int32, (SUBLANES, LANES), 0)
    dsk = dsk_ref[...]
    dirs = (0, 1)

    def col(c):
        return slice(c * LANES, (c + 1) * LANES)

    def init(i, carry):
        rows = pl.ds(pl.multiple_of(i * S5_BLOCK, S5_BLOCK), S5_BLOCK)
        y_ref[rows, :] = dsk * u_ref[rows, :]
        return carry

    lax.fori_loop(0, n_blk, init, 0)
    carry_sc[...] = jnp.zeros_like(carry_sc)

    def block_pair(blk, carry):
        rows0 = (pl.multiple_of(blk * S5_BLOCK, S5_BLOCK),
                 pl.multiple_of((n_blk - 1 - blk) * S5_BLOCK, S5_BLOCK))

        for d in dirs:
            for s in range(SUBLANES):
                lhs_sc[d, pl.ds(s, S5_SUB, stride=SUBLANES), :] = (
                    u_ref[pl.ds(rows0[d] + s * S5_SUB, S5_SUB), :])
            bu = jnp.dot(lhs_sc[d].astype(BF16), bm_ref[d], preferred_element_type=F32)
            for c in range(2 * n_col):
                x_sc[d, c] = bu[:, col(c)]

        a_cols = [[(a_ref[d, 0, 0, :, col(c)], a_ref[d, 0, 1, :, col(c)]) for c in range(n_col)]
                  for d in dirs]
        state = [[None] * (2 * n_col) for _ in dirs]
        for jj in range(S5_SUB):
            for d in dirs:
                j = jj if d == 0 else S5_SUB - 1 - jj
                r8 = slice(j * SUBLANES, (j + 1) * SUBLANES)
                for c in range(n_col):
                    if jj == 0:
                        nr, ni = x_sc[d, c, r8, :], x_sc[d, n_col + c, r8, :]
                    else:
                        xr, xi = state[d][2 * c], state[d][2 * c + 1]
                        ar, ai = a_cols[d][c]
                        nr = ar * xr - ai * xi + x_sc[d, c, r8, :]
                        ni = ar * xi + ai * xr + x_sc[d, n_col + c, r8, :]
                        x_sc[d, c, r8, :] = nr
                        x_sc[d, n_col + c, r8, :] = ni
                    state[d][2 * c], state[d][2 * c + 1] = nr, ni

        cfix = [[None] * n_col for _ in dirs]
        for d in dirs:
            fwd = d == 0
            for c in range(n_col):
                er, ei = state[d][2 * c], state[d][2 * c + 1]
                if fwd:
                    cin_r = carry_sc[d, 0, SUBLANES - 1:SUBLANES, col(c)]
                    cin_i = carry_sc[d, 1, SUBLANES - 1:SUBLANES, col(c)]
                    cr = jnp.where(row == 0, cin_r, pltpu.roll(er, 1, 0))
                    ci = jnp.where(row == 0, cin_i, pltpu.roll(ei, 1, 0))
                else:
                    cin_r = carry_sc[d, 0, 0:1, col(c)]
                    cin_i = carry_sc[d, 1, 0:1, col(c)]
                    cr = jnp.where(row == SUBLANES - 1, cin_r, pltpu.roll(er, SUBLANES - 1, 0))
                    ci = jnp.where(row == SUBLANES - 1, cin_i, pltpu.roll(ei, SUBLANES - 1, 0))
                for k, pw in ((1, 1), (2, 2), (4, 3)):
                    pr, pi = a_ref[d, pw, 0, :, col(c)], a_ref[d, pw, 1, :, col(c)]
                    if fwd:
                        sr = jnp.where(row >= k, pltpu.roll(cr, k, 0), 0.0)
                        si = jnp.where(row >= k, pltpu.roll(ci, k, 0), 0.0)
                    else:
                        sr = jnp.where(row < SUBLANES - k, pltpu.roll(cr, SUBLANES - k, 0), 0.0)
                        si = jnp.where(row < SUBLANES - k, pltpu.roll(ci, SUBLANES - k, 0), 0.0)
                    cr, ci = cr + pr * sr - pi * si, ci + pr * si + pi * sr
                pr, pi = a_ref[d, 1, 0, :, col(c)], a_ref[d, 1, 1, :, col(c)]
                carry_sc[d, 0, :, col(c)] = pr * cr - pi * ci + er
                carry_sc[d, 1, :, col(c)] = pr * ci + pi * cr + ei
                cfix[d][c] = (cr, ci)

        for j in range(S5_SUB):
            r8 = slice(j * SUBLANES, (j + 1) * SUBLANES)
            for d in dirs:
                pj = j if d == 0 else S5_SUB - 1 - j
                for c in range(n_col):
                    pr = pow_ref[d, 0, pj, :, col(c)]
                    pi = pow_ref[d, 1, pj, :, col(c)]
                    cr, ci = cfix[d][c]
                    x_sc[d, c, r8, :] = x_sc[d, c, r8, :] + (pr * cr - pi * ci)
                    x_sc[d, n_col + c, r8, :] = x_sc[d, n_col + c, r8, :] + (pr * ci + pi * cr)

        for d in dirs:
            xcat = jnp.concatenate([x_sc[d, c] for c in range(2 * n_col)], axis=-1).astype(BF16)
            ysc[d] = jnp.dot(xcat, cm_ref[d], preferred_element_type=F32)
            for s in range(SUBLANES):
                rows = pl.ds(rows0[d] + s * S5_SUB, S5_SUB)
                y_ref[rows, :] = y_ref[rows, :] + ysc[d, pl.ds(s, S5_SUB, stride=SUBLANES), :]
        return carry

    lax.fori_loop(0, n_blk, block_pair, 0)


def _s5(u, b_mat, c_mat, a_tab, pow_tab, d_skip):
    bsz, seq, w = u.shape
    n_gb = w // LANES
    width = b_mat.shape[-1] // 2
    assert seq % S5_BLOCK == 0 and width % LANES == 0
    n_col = width // LANES
    return pl.pallas_call(
        functools.partial(_s5_kernel, seq=seq, width=width),
        grid=(bsz, n_gb),
        in_specs=[
            pl.BlockSpec((None, seq, LANES), lambda b, g: (b, 0, g)),
            pl.BlockSpec((2, None, LANES, 2 * width), lambda b, g: (0, g, 0, 0)),
            pl.BlockSpec((2, None, 2 * width, LANES), lambda b, g: (0, g, 0, 0)),
            pl.BlockSpec((2, None, 4, 2, SUBLANES, width), lambda b, g: (0, g, 0, 0, 0, 0)),
            pl.BlockSpec((2, None, 2, S5_SUB, SUBLANES, width), lambda b, g: (0, g, 0, 0, 0, 0)),
            pl.BlockSpec((1, LANES), lambda b, g: (0, g)),
        ],
        out_specs=pl.BlockSpec((None, seq, LANES), lambda b, g: (b, 0, g)),
        out_shape=jax.ShapeDtypeStruct((bsz, seq, w), F32),
        scratch_shapes=[
            pltpu.VMEM((2, S5_BLOCK, LANES), F32),
            pltpu.VMEM((2, 2 * n_col, S5_BLOCK, LANES), F32),
            pltpu.VMEM((2, S5_BLOCK, LANES), F32),
            pltpu.VMEM((2, 2, SUBLANES, width), F32),
        ],
        compiler_params=_cparams("parallel", "parallel"),
    )(u, b_mat, c_mat, a_tab, pow_tab, d_skip.reshape(1, w))


def _mix_out_kernel(attn_ref, g_ref, x_ref, mod_ref, wglu_ref, bglu_ref, gattn_ref, gssm_ref,
                    wout_ref, lng_ref, lnb_ref, o_ref, *, alpha, width, n_sub):
    sub = x_ref.shape[0] // n_sub
    gate = mod_ref[2:3, :]
    for h in range(n_sub):
        rows = slice(h * sub, (h + 1) * sub)
        g = jax.nn.gelu(g_ref[rows, :])
        z = jnp.dot(g.astype(BF16), wglu_ref[...], preferred_element_type=F32) + bglu_ref[...]
        ssm = g * jax.nn.sigmoid(z)
        ra = _rms_norm(attn_ref[rows, :], gattn_ref[...]).astype(BF16)
        rs = _rms_norm(ssm, gssm_ref[...]).astype(BF16)
        mix = (jnp.dot(ra, wout_ref[0:width, :], preferred_element_type=F32)
               + jnp.dot(rs, wout_ref[width:2 * width, :], preferred_element_type=F32))
        y = alpha * x_ref[rows, :] + (1.0 + gate) * mix
        o_ref[rows, :] = _layer_norm(y, lng_ref[...], lnb_ref[...])


def _mix_out(attn, g, x, mod, w_glu, b_glu, g_attn, g_ssm, w_out, ln_g, ln_b, alpha):
    bsz, seq, d_model = x.shape
    width = d_model // 2
    tm = min(MIX_ROWS, seq)
    assert seq % tm == 0 and tm % MIX_SUB_ROWS == 0
    row = lambda b, i: (b, i, 0)
    fixed = lambda b, i: (0, 0)
    return pl.pallas_call(
        functools.partial(_mix_out_kernel, alpha=alpha, width=width, n_sub=tm // MIX_SUB_ROWS),
        grid=(bsz, seq // tm),
        in_specs=[
            pl.BlockSpec((None, tm, width), row),
            pl.BlockSpec((None, tm, width), row),
            pl.BlockSpec((None, tm, d_model), row),
            pl.BlockSpec((None, 6, d_model), lambda b, i: (b, 0, 0)),
            pl.BlockSpec((width, width), fixed, pipeline_mode=pl.Buffered(1)),
            pl.BlockSpec((1, width), fixed),
            pl.BlockSpec((1, width), fixed),
            pl.BlockSpec((1, width), fixed),
            pl.BlockSpec((d_model, d_model), fixed, pipeline_mode=pl.Buffered(1)),
            pl.BlockSpec((1, d_model), fixed),
            pl.BlockSpec((1, d_model), fixed),
        ],
        out_specs=pl.BlockSpec((None, tm, d_model), row),
        out_shape=jax.ShapeDtypeStruct((bsz, seq, d_model), F32),
        compiler_params=_cparams("parallel", "parallel"),
    )(attn, g, x, mod, w_glu, b_glu.reshape(1, width), g_attn.reshape(1, width),
      g_ssm.reshape(1, width), w_out, ln_g.reshape(1, d_model), ln_b.reshape(1, d_model))


def _mlp_kernel(x_ref, mod_ref, w1_ref, w2_ref, lng_ref, lnb_ref, o_ref, h_sc, acc_sc, *, alpha):
    kf = pl.program_id(2)

    @pl.when(kf == 0)
    def _():
        sh = mod_ref[3:4, :]
        sc = mod_ref[4:5, :]
        h_sc[...] = (x_ref[...] * (1.0 + sc) + sh).astype(BF16)
        acc_sc[...] = jnp.zeros_like(acc_sc)

    a = jnp.maximum(jnp.dot(h_sc[...], w1_ref[...], preferred_element_type=F32), 0.0)
    acc_sc[...] += jnp.dot((a * a).astype(BF16), w2_ref[...], preferred_element_type=F32)

    @pl.when(kf == pl.num_programs(2) - 1)
    def _():
        gate = mod_ref[5:6, :]
        y = alpha * x_ref[...] + (1.0 + gate) * acc_sc[...]
        o_ref[...] = _layer_norm(y, lng_ref[...], lnb_ref[...])


def _mlp(x, mod, w1, w2, ln_g, ln_b, alpha):
    bsz, seq, d_model = x.shape
    d_ff = w1.shape[1]
    tm = min(MLP_ROWS, seq)
    tf = min(MLP_FF_CHUNK, d_ff)
    assert seq % tm == 0 and d_ff % tf == 0
    return pl.pallas_call(
        functools.partial(_mlp_kernel, alpha=alpha),
        grid=(bsz, seq // tm, d_ff // tf),
        in_specs=[
            pl.BlockSpec((None, tm, d_model), lambda b, i, k: (b, i, 0)),
            pl.BlockSpec((None, 6, d_model), lambda b, i, k: (b, 0, 0)),
            pl.BlockSpec((d_model, tf), lambda b, i, k: (0, k)),
            pl.BlockSpec((tf, d_model), lambda b, i, k: (k, 0)),
            pl.BlockSpec((1, d_model), lambda b, i, k: (0, 0)),
            pl.BlockSpec((1, d_model), lambda b, i, k: (0, 0)),
        ],
        out_specs=pl.BlockSpec((None, tm, d_model), lambda b, i, k: (b, i, 0)),
        out_shape=jax.ShapeDtypeStruct((bsz, seq, d_model), F32),
        scratch_shapes=[
            pltpu.VMEM((tm, d_model), BF16),
            pltpu.VMEM((tm, d_model), F32),
        ],
        compiler_params=_cparams("parallel", "parallel", "arbitrary"),
    )(x, mod, w1, w2, ln_g.reshape(1, d_model), ln_b.reshape(1, d_model))


def kernel(x, c, w_ada, b_ada, w_in, ssm_lam_re, ssm_lam_im, ssm_log_step, ssm_b_re, ssm_b_im,
           ssm_c_re, ssm_c_im, ssm_d, w_glu, b_glu, g_attn, g_ssm, w_out, ln1_g, ln1_b,
           w_mlp1, w_mlp2, ln2_g, ln2_b):
    depth = w_ada.shape[0]
    d_model = x.shape[-1]
    n_heads = (d_model // 2) // HEAD_DIM
    alpha = float((2 * depth) ** 0.25)

    mods = _ada_mods(c, w_ada, b_ada)
    b_mat, c_mat, a_tab, pow_tab = _s5_prepare(ssm_lam_re, ssm_lam_im, ssm_log_step,
                                               ssm_b_re, ssm_b_im, ssm_c_re, ssm_c_im)
    bias_table = _attn_bias_table(n_heads)
    w_in_h, w_glu_h, w_out_h = w_in.astype(BF16), w_glu.astype(BF16), w_out.astype(BF16)
    w1_h, w2_h = w_mlp1.astype(BF16), w_mlp2.astype(BF16)

    for l in range(depth):
        qkv1, qkv4, qkv16, u = _in_proj(x, mods[l], w_in_h[l])
        attn = _attention(qkv1, qkv4, qkv16, bias_table)
        g = _s5(u, b_mat[l], c_mat[l], a_tab[l], pow_tab[l], ssm_d[l])
        x = _mix_out(attn, g, x, mods[l], w_glu_h[l], b_glu[l], g_attn[l], g_ssm[l], w_out_h[l],
                     ln1_g[l], ln1_b[l], alpha)
        x = _mlp(x, mods[l], w1_h[l], w2_h[l], ln2_g[l], ln2_b[l], alpha)
    return x
```

```python
import functools

import jax
import jax.numpy as jnp
from jax import lax
from jax.experimental import pallas as pl
from jax.experimental.pallas import tpu as pltpu

F32 = jnp.float32
BF16 = jnp.bfloat16

LANES = 128
SUBLANES = 8
MXU_COLS = 256
VMEM_LIMIT_BYTES = 56 * 1024 * 1024

HEAD_DIM = 64
ATTN_DILATIONS = (1, 4, 16)
ATTN_HALF = 64
ATTN_Q = 128
ATTN_K = ATTN_Q + 2 * ATTN_HALF
ATTN_TILE = ATTN_Q * max(ATTN_DILATIONS)
SSM_GROUP = 16
S5_GROUPS_PER_STEP = LANES // SSM_GROUP
S5_CHUNK = 64
S5_TILE = 512
LN_EPS = 1e-5
NEG_BIG = -1e30

IN_PROJ_ROWS = 512
MIX_ROWS = 512
MIX_SUB_ROWS = 256
MLP_ROWS = 512
MLP_FF_CHUNK = 1024
ADA_COLS = 1024


def _cparams(*sem):
    return pltpu.CompilerParams(dimension_semantics=sem, vmem_limit_bytes=VMEM_LIMIT_BYTES)


def _layer_norm(y, g, b):
    mu = jnp.mean(y, axis=-1, keepdims=True)
    yc = y - mu
    var = jnp.mean(yc * yc, axis=-1, keepdims=True)
    return yc * lax.rsqrt(var + LN_EPS) * g + b


def _rms_norm(y, g):
    return y * lax.rsqrt(jnp.mean(y * y, axis=-1, keepdims=True) + LN_EPS) * g


def _ada_kernel(c_ref, w_ref, b_ref, o_ref):
    cond = jax.nn.silu(c_ref[...]).astype(BF16)
    o_ref[...] = jnp.dot(cond, w_ref[...].astype(BF16), preferred_element_type=F32) + b_ref[...]


def _ada_mods(c, w_ada, b_ada):
    depth, d_model, n_out = w_ada.shape
    bsz = c.shape[0]
    rows = -(-bsz // SUBLANES) * SUBLANES
    c_pad = jnp.zeros((rows, d_model), F32).at[:bsz].set(c)
    tn = min(ADA_COLS, d_model)
    assert n_out % tn == 0
    out = pl.pallas_call(
        _ada_kernel,
        grid=(depth, n_out // tn),
        in_specs=[
            pl.BlockSpec((rows, d_model), lambda l, n: (0, 0)),
            pl.BlockSpec((None, d_model, tn), lambda l, n: (l, 0, n)),
            pl.BlockSpec((None, 1, tn), lambda l, n: (l, 0, n)),
        ],
        out_specs=pl.BlockSpec((None, rows, tn), lambda l, n: (l, 0, n)),
        out_shape=jax.ShapeDtypeStruct((depth, rows, n_out), F32),
        compiler_params=_cparams("parallel", "parallel"),
    )(c_pad, w_ada, b_ada.reshape(depth, 1, n_out))
    return out[:, :bsz].reshape(depth, bsz, 6, d_model)


def _in_proj_kernel(x_ref, mod_ref, w_ref, qkv1_ref, qkv4_ref, qkv16_ref, u_ref, h_sc, acc_sc,
                    acc4_sc, *, tm, n_slabs):
    sh = mod_ref[0:1, :]
    sc = mod_ref[1:2, :]
    h_sc[...] = (x_ref[...] * (1.0 + sc) + sh).astype(BF16)

    per = min(MXU_COLS // LANES, n_slabs)
    for idx, c0 in enumerate(range(0, 4 * n_slabs, per)):
        wide = slice(c0 * LANES, (c0 + per) * LANES)
        acc = jnp.dot(h_sc[...], w_ref[:, wide], preferred_element_type=F32)
        if c0 >= 3 * n_slabs:
            u_ref[:, (c0 - 3 * n_slabs) * LANES:(c0 - 3 * n_slabs + per) * LANES] = acc
            continue
        if c0 < n_slabs:
            acc = acc * (HEAD_DIM ** -0.5)
        qkv1_ref[:, wide] = acc.astype(BF16)
        for c in range(c0, c0 + per):
            cols = slice(c * LANES, (c + 1) * LANES)
            slab = (idx % 2) * per + (c - c0)
            acc_sc[slab] = acc[:, (c - c0) * LANES:(c - c0 + 1) * LANES]
            for r4 in range(4):
                cls = acc_sc[slab, pl.ds(r4, tm // 4, stride=4), :]
                qkv4_ref[r4, :, cols] = cls.astype(BF16)
                acc4_sc[slab * 4 + r4] = cls
                for rr in range(4):
                    qkv16_ref[4 * rr + r4, :, cols] = (
                        acc4_sc[slab * 4 + r4, pl.ds(rr, tm // 16, stride=4), :].astype(BF16))


def _in_proj(x, mod, w_in_bf16, layer):
    bsz, seq, d_model = x.shape
    width = d_model // 2
    assert w_in_bf16.shape[1:] == (d_model, 4 * width) and width % LANES == 0
    tm = min(IN_PROJ_ROWS, seq)
    assert seq % tm == 0 and tm % (16 * 16) == 0
    n_slabs = width // LANES
    per = min(MXU_COLS // LANES, n_slabs)
    return pl.pallas_call(
        functools.partial(_in_proj_kernel, tm=tm, n_slabs=n_slabs),
        grid=(bsz, seq // tm),
        in_specs=[
            pl.BlockSpec((None, tm, d_model), lambda b, i: (b, i, 0)),
            pl.BlockSpec((None, 6, d_model), lambda b, i: (b, 0, 0)),
            pl.BlockSpec((None, d_model, 4 * width), lambda b, i: (layer, 0, 0),
                         pipeline_mode=pl.Buffered(1)),
        ],
        out_specs=[
            pl.BlockSpec((None, tm, 3 * width), lambda b, i: (b, i, 0)),
            pl.BlockSpec((None, 4, tm // 4, 3 * width), lambda b, i: (b, 0, i, 0)),
            pl.BlockSpec((None, 16, tm // 16, 3 * width), lambda b, i: (b, 0, i, 0)),
            pl.BlockSpec((None, tm, width), lambda b, i: (b, i, 0)),
        ],
        out_shape=[
            jax.ShapeDtypeStruct((bsz, seq, 3 * width), BF16),
            jax.ShapeDtypeStruct((bsz, 4, seq // 4, 3 * width), BF16),
            jax.ShapeDtypeStruct((bsz, 16, seq // 16, 3 * width), BF16),
            jax.ShapeDtypeStruct((bsz, seq, width), F32),
        ],
        scratch_shapes=[
            pltpu.VMEM((tm, d_model), BF16),
            pltpu.VMEM((2 * per, tm, LANES), F32),
            pltpu.VMEM((2 * per * 4, tm // 4, LANES), F32),
        ],
        compiler_params=_cparams("parallel", "parallel"),
    )(x, mod, w_in_bf16)


def _attn_bias_table(n_heads):
    slopes = jnp.exp2(-8.0 * jnp.arange(1, n_heads + 1, dtype=F32) / n_heads)
    qi = jnp.arange(ATTN_Q)[:, None]
    kc = jnp.arange(ATTN_K)[None, :]
    per_pattern = []
    for d in ATTN_DILATIONS:
        per_variant = []
        for v in range(3):
            rel = jnp.abs(kc - qi - ATTN_HALF * v)
            valid = rel <= ATTN_HALF
            dist = (rel * d).astype(F32)
            bias = jnp.where(valid[None], -slopes[:, None, None] * dist[None], NEG_BIG)
            per_variant.append(bias.reshape(n_heads // 2, 2, ATTN_Q, ATTN_K))
        per_pattern.append(jnp.stack(per_variant, axis=1))
    return jnp.stack(per_pattern, axis=1)


def _attn_kernel(q1_ref, k1_ref, v1_ref, q4_ref, k4_ref, v4_ref, q16_ref, k16_ref, v16_ref,
                 bias_ref, o_ref, acc_sc, m_sc, l_sc, *, seq):
    t = pl.program_id(2)
    lane = lax.broadcasted_iota(jnp.int32, (1, LANES), 1)
    head_a = lane < HEAD_DIM

    def block(p, load_q, load_k, load_v, qs, length):
        ks = jnp.clip(qs - ATTN_HALF, 0, length - ATTN_K)
        variant = (qs - ks) // ATTN_HALF
        ks = pl.multiple_of(ks, ATTN_HALF)
        q = load_q(qs)
        k = load_k(ks)
        v = load_v(ks)
        res = []
        for h in range(2):
            mask = head_a if h == 0 else jnp.logical_not(head_a)
            qh = jnp.where(mask, q, jnp.zeros_like(q))
            s = lax.dot_general(qh, k, (((1,), (1,)), ((), ())), preferred_element_type=F32)
            s = s + bias_ref[p, variant, h]
            m = jnp.max(s, axis=-1, keepdims=True)
            e = jnp.exp(s - m)
            l = jnp.sum(e, axis=-1, keepdims=True)
            o = jnp.dot(e.astype(BF16), v, preferred_element_type=F32)
            res.append((o, m, l))
        (o0, m0, l0), (o1, m1, l1) = res
        return (jnp.where(head_a, o0, o1), jnp.where(head_a, m0, m1), jnp.where(head_a, l0, l1))

    def body(j):
        qs = pl.multiple_of(t * ATTN_TILE + j * ATTN_Q, ATTN_Q)
        acc, m, l = block(0,
                          lambda s: q1_ref[pl.ds(s, ATTN_Q), :],
                          lambda s: k1_ref[pl.ds(s, ATTN_K), :],
                          lambda s: v1_ref[pl.ds(s, ATTN_K), :],
                          qs, seq)
        rows = pl.ds(j * ATTN_Q, ATTN_Q)
        acc_sc[0, rows, :] = acc
        m_sc[0, rows, :] = m
        l_sc[0, rows, :] = l
        r = j // 4
        jj = j % 4
        qs = pl.multiple_of(t * (ATTN_TILE // 4) + jj * ATTN_Q, ATTN_Q)
        acc, m, l = block(1,
                          lambda s: q4_ref[r, pl.ds(s, ATTN_Q), :],
                          lambda s: k4_ref[r, pl.ds(s, ATTN_K), :],
                          lambda s: v4_ref[r, pl.ds(s, ATTN_K), :],
                          qs, seq // 4)
        rows = pl.ds(r + 4 * ATTN_Q * jj, ATTN_Q, stride=4)
        acc_sc[1, rows, :] = acc
        m_sc[1, rows, :] = m
        l_sc[1, rows, :] = l
        qs = pl.multiple_of(t * ATTN_Q, ATTN_Q)
        acc, m, l = block(2,
                          lambda s: q16_ref[j, pl.ds(s, ATTN_Q), :],
                          lambda s: k16_ref[j, pl.ds(s, ATTN_K), :],
                          lambda s: v16_ref[j, pl.ds(s, ATTN_K), :],
                          qs, seq // 16)
        rows = pl.ds(j, ATTN_Q, stride=16)
        acc_sc[2, rows, :] = acc
        m_sc[2, rows, :] = m
        l_sc[2, rows, :] = l

    for j in range(ATTN_TILE // ATTN_Q):
        body(j)

    def combine(i, carry):
        rows = pl.ds(pl.multiple_of(i * 256, 256), 256)
        ms = [m_sc[p, rows, :] for p in range(3)]
        m_all = jnp.maximum(jnp.maximum(ms[0], ms[1]), ms[2])
        num = jnp.zeros((256, LANES), F32)
        den = jnp.zeros((256, LANES), F32)
        for p in range(3):
            w = jnp.exp(ms[p] - m_all)
            num = num + w * acc_sc[p, rows, :]
            den = den + w * l_sc[p, rows, :]
        o_ref[rows, :] = num / den
        return carry

    lax.fori_loop(0, ATTN_TILE // 256, combine, 0)


def _attention(qkv1, qkv4, qkv16, bias_table):
    bsz, seq, w3 = qkv1.shape
    width = w3 // 3
    n_pairs = width // LANES
    assert seq % ATTN_TILE == 0 and seq // 16 >= ATTN_K
    col = lambda which: (lambda b, hp, t: (b, 0, which * n_pairs + hp))
    col4 = lambda which: (lambda b, hp, t: (b, 0, 0, which * n_pairs + hp))
    specs = []
    for d, cmap in ((1, col), (4, col4), (16, col4)):
        for which in range(3):
            if d == 1:
                specs.append(pl.BlockSpec((None, seq, LANES), cmap(which)))
            else:
                specs.append(pl.BlockSpec((None, d, seq // d, LANES), cmap(which)))
    specs.append(pl.BlockSpec((None, 3, 3, 2, ATTN_Q, ATTN_K), lambda b, hp, t: (hp, 0, 0, 0, 0, 0)))
    return pl.pallas_call(
        functools.partial(_attn_kernel, seq=seq),
        grid=(bsz, n_pairs, seq // ATTN_TILE),
        in_specs=specs,
        out_specs=pl.BlockSpec((None, ATTN_TILE, LANES), lambda b, hp, t: (b, t, hp)),
        out_shape=jax.ShapeDtypeStruct((bsz, seq, width), F32),
        scratch_shapes=[pltpu.VMEM((3, ATTN_TILE, LANES), F32)] * 3,
        compiler_params=_cparams("parallel", "parallel", "arbitrary"),
    )(qkv1, qkv1, qkv1, qkv4, qkv4, qkv4, qkv16, qkv16, qkv16, bias_table)


def _s5_prepare(lam_re, lam_im, log_step, b_re, b_im, c_re, c_im, bsz):
    depth, _, n_groups, n_state = lam_re.shape
    gps = S5_GROUPS_PER_STEP
    n_gb = n_groups // gps
    width = gps * n_state
    step = jnp.exp(log_step)[..., None]
    mag = jnp.exp(lam_re * step)
    a_re, a_im = mag * jnp.cos(lam_im * step), mag * jnp.sin(lam_im * step)
    den = lam_re * lam_re + lam_im * lam_im
    coef_re = ((a_re - 1.0) * lam_re + a_im * lam_im) / den
    coef_im = (a_im * lam_re - (a_re - 1.0) * lam_im) / den
    bb_re = coef_re[..., None] * b_re - coef_im[..., None] * b_im
    bb_im = coef_re[..., None] * b_im + coef_im[..., None] * b_re
    eye = jnp.eye(gps, dtype=F32)

    def block_diag_in(bb):
        t = bb.reshape(depth, 2, n_gb, gps, n_state, SSM_GROUP).transpose(0, 1, 2, 3, 5, 4)
        full = t[:, :, :, :, :, None, :] * eye[None, None, None, :, None, :, None]
        return full.reshape(depth, 2, n_gb, gps * SSM_GROUP, width)

    def block_diag_out(cc):
        t = cc.reshape(depth, 2, n_gb, gps, SSM_GROUP, n_state).transpose(0, 1, 2, 3, 5, 4)
        full = t[:, :, :, :, :, None, :] * eye[None, None, None, :, None, :, None]
        return full.reshape(depth, 2, n_gb, width, gps * SSM_GROUP)

    b_mat = jnp.concatenate([block_diag_in(bb_re), block_diag_in(bb_im)], axis=-1)
    b_mat = b_mat.transpose(0, 2, 1, 3, 4).reshape(depth, n_gb, 2 * LANES, 2 * width).astype(BF16)
    c_mat = jnp.concatenate([block_diag_out(c_re), -block_diag_out(c_im)], axis=-2)
    c_mat = c_mat.transpose(0, 2, 3, 1, 4).reshape(depth, n_gb, 2 * width, 2 * LANES).astype(BF16)

    def per_sublane(t):
        t = t.reshape(depth, 2, n_gb, 1, width).transpose(0, 2, 1, 3, 4)
        return jnp.broadcast_to(t, (depth, n_gb, 2, bsz, width)).reshape(depth, n_gb, 2 * bsz, width)

    a_tab = jnp.stack([per_sublane(a_re), per_sublane(a_im)], axis=2)
    return b_mat, c_mat, a_tab


def _reverse_rows(x):
    n_grp = x.shape[0] // SUBLANES
    row = lax.broadcasted_iota(jnp.int32, (SUBLANES, LANES), 0)
    out = []
    for g in reversed(range(n_grp)):
        v = x[g * SUBLANES:(g + 1) * SUBLANES, :]
        v = jnp.where((row & 1) == 0, pltpu.roll(v, 7, 0), pltpu.roll(v, 1, 0))
        v = jnp.where((row & 2) == 0, pltpu.roll(v, 6, 0), pltpu.roll(v, 2, 0))
        out.append(pltpu.roll(v, 4, 0))
    return jnp.concatenate(out, axis=0)


def _s5_kernel(uf_ref, ub_ref, bm_ref, cm_ref, a_ref, yf_ref, yb_ref,
               lhs_sc, x_sc, ysc, st_sc, *, tile, width, bsz):
    n_col = width // LANES
    n_chunk = tile // S5_CHUNK

    @pl.when(pl.program_id(1) == 0)
    def _():
        st_sc[...] = jnp.zeros_like(st_sc)
        lhs_sc[...] = jnp.zeros_like(lhs_sc)

    def col(c):
        return slice(c * LANES, (c + 1) * LANES)

    def project_in(k):
        slot = k % 2
        f_rows = slice(k * S5_CHUNK, (k + 1) * S5_CHUNK)
        b_rows = slice(tile - (k + 1) * S5_CHUNK, tile - k * S5_CHUNK)
        for b in range(bsz):
            lhs_sc[slot, 0, pl.ds(b, S5_CHUNK, stride=SUBLANES), :] = uf_ref[b, f_rows, :]
            lhs_sc[slot, 1, pl.ds(bsz + b, S5_CHUNK, stride=SUBLANES), :] = (
                _reverse_rows(ub_ref[b, b_rows, :]))
        lhs = jnp.concatenate([lhs_sc[slot, 0], lhs_sc[slot, 1]], axis=-1).astype(BF16)
        bu = jnp.dot(lhs, bm_ref[...], preferred_element_type=F32)
        for c in range(2 * n_col):
            x_sc[slot, c] = bu[:, col(c)]

    def scan(k):
        slot = k % 2
        a_cols = [(a_ref[0, :, col(c)], a_ref[1, :, col(c)]) for c in range(n_col)]
        state = [(st_sc[0, :, col(c)], st_sc[1, :, col(c)]) for c in range(n_col)]
        for t in range(S5_CHUNK):
            r8 = slice(t * SUBLANES, (t + 1) * SUBLANES)
            for c in range(n_col):
                xr, xi = state[c]
                ar, ai = a_cols[c]
                nr = ar * xr - ai * xi + x_sc[slot, c, r8, :]
                ni = ar * xi + ai * xr + x_sc[slot, n_col + c, r8, :]
                x_sc[slot, c, r8, :] = nr
                x_sc[slot, n_col + c, r8, :] = ni
                state[c] = (nr, ni)
        for c in range(n_col):
            st_sc[0, :, col(c)] = state[c][0]
            st_sc[1, :, col(c)] = state[c][1]

    def project_out(k):
        slot = k % 2
        f_rows = slice(k * S5_CHUNK, (k + 1) * S5_CHUNK)
        b_rows = slice(tile - (k + 1) * S5_CHUNK, tile - k * S5_CHUNK)
        xcat = jnp.concatenate([x_sc[slot, c] for c in range(2 * n_col)], axis=-1).astype(BF16)
        y = jnp.dot(xcat, cm_ref[...], preferred_element_type=F32)
        ysc[0] = y[:, 0:LANES]
        ysc[1] = y[:, LANES:2 * LANES]
        for b in range(bsz):
            yf_ref[b, f_rows, :] = ysc[0, pl.ds(b, S5_CHUNK, stride=SUBLANES), :]
            yb_ref[b, b_rows, :] = _reverse_rows(
                ysc[1, pl.ds(bsz + b, S5_CHUNK, stride=SUBLANES), :])

    project_in(0)
    for k in range(n_chunk):
        if k + 1 < n_chunk:
            project_in(k + 1)
        scan(k)
        project_out(k)


def _s5(u, b_mat, c_mat, a_tab, layer):
    bsz, seq, w = u.shape
    n_gb = w // LANES
    width = b_mat.shape[-1] // 2
    assert 2 * bsz == SUBLANES and seq % S5_TILE == 0 and width % LANES == 0
    n_t = seq // S5_TILE
    n_col = width // LANES
    rows_c = S5_CHUNK * SUBLANES
    fwd = lambda g, i: (0, i, g)
    bwd = lambda g, i: (0, n_t - 1 - i, g)
    slab = lambda g, i: (layer, g, 0, 0)
    out_shape = jax.ShapeDtypeStruct((bsz, seq, w), F32)
    return pl.pallas_call(
        functools.partial(_s5_kernel, tile=S5_TILE, width=width, bsz=bsz),
        grid=(n_gb, n_t),
        in_specs=[
            pl.BlockSpec((bsz, S5_TILE, LANES), fwd),
            pl.BlockSpec((bsz, S5_TILE, LANES), bwd),
            pl.BlockSpec((None, None, 2 * LANES, 2 * width), slab),
            pl.BlockSpec((None, None, 2 * width, 2 * LANES), slab),
            pl.BlockSpec((None, None, 2, SUBLANES, width), lambda g, i: (layer, g, 0, 0, 0)),
        ],
        out_specs=[pl.BlockSpec((bsz, S5_TILE, LANES), fwd),
                   pl.BlockSpec((bsz, S5_TILE, LANES), bwd)],
        out_shape=[out_shape, out_shape],
        scratch_shapes=[
            pltpu.VMEM((2, 2, rows_c, LANES), F32),
            pltpu.VMEM((2, 2 * n_col, rows_c, LANES), F32),
            pltpu.VMEM((2, rows_c, LANES), F32),
            pltpu.VMEM((2, SUBLANES, width), F32),
        ],
        compiler_params=_cparams("parallel", "arbitrary"),
    )(u, u, b_mat, c_mat, a_tab)


def _mix_out_kernel(attn_ref, yf_ref, yb_ref, u_ref, dsk_ref, x_ref, mod_ref, wglu_ref, bglu_ref,
                    gattn_ref, gssm_ref, wout_ref, lng_ref, lnb_ref, o_ref, *, alpha, width, n_sub):
    sub = x_ref.shape[0] // n_sub
    gate = mod_ref[2:3, :]
    for h in range(n_sub):
        rows = slice(h * sub, (h + 1) * sub)
        g = jax.nn.gelu(yf_ref[rows, :] + yb_ref[rows, :] + dsk_ref[...] * u_ref[rows, :])
        z = jnp.dot(g.astype(BF16), wglu_ref[...], preferred_element_type=F32) + bglu_ref[...]
        ssm = g * jax.nn.sigmoid(z)
        ra = _rms_norm(attn_ref[rows, :], gattn_ref[...]).astype(BF16)
        rs = _rms_norm(ssm, gssm_ref[...]).astype(BF16)
        mix = (jnp.dot(ra, wout_ref[0:width, :], preferred_element_type=F32)
               + jnp.dot(rs, wout_ref[width:2 * width, :], preferred_element_type=F32))
        y = alpha * x_ref[rows, :] + (1.0 + gate) * mix
        o_ref[rows, :] = _layer_norm(y, lng_ref[...], lnb_ref[...])


def _mix_out(attn, yf, yb, u, d_skip, x, mod, w_glu, b_glu, g_attn, g_ssm, w_out, ln_g, ln_b,
             alpha, layer):
    bsz, seq, d_model = x.shape
    width = d_model // 2
    tm = min(MIX_ROWS, seq)
    assert seq % tm == 0 and tm % MIX_SUB_ROWS == 0
    row = lambda b, i: (b, i, 0)
    fixed = lambda b, i: (0, 0)
    return pl.pallas_call(
        functools.partial(_mix_out_kernel, alpha=alpha, width=width, n_sub=tm // MIX_SUB_ROWS),
        grid=(bsz, seq // tm),
        in_specs=[
            pl.BlockSpec((None, tm, width), row),
            pl.BlockSpec((None, tm, width), row),
            pl.BlockSpec((None, tm, width), row),
            pl.BlockSpec((None, tm, width), row),
            pl.BlockSpec((1, width), fixed),
            pl.BlockSpec((None, tm, d_model), row),
            pl.BlockSpec((None, 6, d_model), lambda b, i: (b, 0, 0)),
            pl.BlockSpec((None, width, width), lambda b, i: (layer, 0, 0),
                         pipeline_mode=pl.Buffered(1)),
            pl.BlockSpec((1, width), fixed),
            pl.BlockSpec((1, width), fixed),
            pl.BlockSpec((1, width), fixed),
            pl.BlockSpec((None, d_model, d_model), lambda b, i: (layer, 0, 0),
                         pipeline_mode=pl.Buffered(1)),
            pl.BlockSpec((1, d_model), fixed),
            pl.BlockSpec((1, d_model), fixed),
        ],
        out_specs=pl.BlockSpec((None, tm, d_model), row),
        out_shape=jax.ShapeDtypeStruct((bsz, seq, d_model), F32),
        compiler_params=_cparams("parallel", "parallel"),
    )(attn, yf, yb, u, d_skip.reshape(1, width), x, mod, w_glu, b_glu.reshape(1, width),
      g_attn.reshape(1, width), g_ssm.reshape(1, width), w_out, ln_g.reshape(1, d_model),
      ln_b.reshape(1, d_model))


def _mlp_kernel(x_ref, mod_ref, w1_ref, w2_ref, lng_ref, lnb_ref, o_ref, h_sc, acc_sc, *, alpha):
    kf = pl.program_id(2)

    @pl.when(kf == 0)
    def _():
        sh = mod_ref[3:4, :]
        sc = mod_ref[4:5, :]
        h_sc[...] = (x_ref[...] * (1.0 + sc) + sh).astype(BF16)
        acc_sc[...] = jnp.zeros_like(acc_sc)

    a = jnp.maximum(jnp.dot(h_sc[...], w1_ref[...], preferred_element_type=F32), 0.0)
    acc_sc[...] += jnp.dot((a * a).astype(BF16), w2_ref[...], preferred_element_type=F32)

    @pl.when(kf == pl.num_programs(2) - 1)
    def _():
        gate = mod_ref[5:6, :]
        y = alpha * x_ref[...] + (1.0 + gate) * acc_sc[...]
        o_ref[...] = _layer_norm(y, lng_ref[...], lnb_ref[...])


def _mlp(x, mod, w1, w2, ln_g, ln_b, alpha, layer):
    bsz, seq, d_model = x.shape
    d_ff = w1.shape[2]
    tm = min(MLP_ROWS, seq)
    tf = min(MLP_FF_CHUNK, d_ff)
    assert seq % tm == 0 and d_ff % tf == 0
    return pl.pallas_call(
        functools.partial(_mlp_kernel, alpha=alpha),
        grid=(bsz, seq // tm, d_ff // tf),
        in_specs=[
            pl.BlockSpec((None, tm, d_model), lambda b, i, k: (b, i, 0)),
            pl.BlockSpec((None, 6, d_model), lambda b, i, k: (b, 0, 0)),
            pl.BlockSpec((None, d_model, tf), lambda b, i, k: (layer, 0, k)),
            pl.BlockSpec((None, tf, d_model), lambda b, i, k: (layer, k, 0)),
            pl.BlockSpec((1, d_model), lambda b, i, k: (0, 0)),
            pl.BlockSpec((1, d_model), lambda b, i, k: (0, 0)),
        ],
        out_specs=pl.BlockSpec((None, tm, d_model), lambda b, i, k: (b, i, 0)),
        out_shape=jax.ShapeDtypeStruct((bsz, seq, d_model), F32),
        scratch_shapes=[
            pltpu.VMEM((tm, d_model), BF16),
            pltpu.VMEM((tm, d_model), F32),
        ],
        compiler_params=_cparams("parallel", "parallel", "arbitrary"),
    )(x, mod, w1, w2, ln_g.reshape(1, d_model), ln_b.reshape(1, d_model))


def kernel(x, c, w_ada, b_ada, w_in, ssm_lam_re, ssm_lam_im, ssm_log_step, ssm_b_re, ssm_b_im,
           ssm_c_re, ssm_c_im, ssm_d, w_glu, b_glu, g_attn, g_ssm, w_out, ln1_g, ln1_b,
           w_mlp1, w_mlp2, ln2_g, ln2_b):
    depth = w_ada.shape[0]
    d_model = x.shape[-1]
    n_heads = (d_model // 2) // HEAD_DIM
    alpha = float((2 * depth) ** 0.25)

    mods = _ada_mods(c, w_ada, b_ada)
    b_mat, c_mat, a_tab = _s5_prepare(ssm_lam_re, ssm_lam_im, ssm_log_step,
                                      ssm_b_re, ssm_b_im, ssm_c_re, ssm_c_im, x.shape[0])
    bias_table = _attn_bias_table(n_heads)
    w_in_h, w_glu_h, w_out_h = w_in.astype(BF16), w_glu.astype(BF16), w_out.astype(BF16)
    w1_h, w2_h = w_mlp1.astype(BF16), w_mlp2.astype(BF16)

    for l in range(depth):
        qkv1, qkv4, qkv16, u = _in_proj(x, mods[l], w_in_h, l)
        attn = _attention(qkv1, qkv4, qkv16, bias_table)
        yf, yb = _s5(u, b_mat, c_mat, a_tab, l)
        x = _mix_out(attn, yf, yb, u, ssm_d[l], x, mods[l], w_glu_h, b_glu[l], g_attn[l], g_ssm[l],
                     w_out_h, ln1_g[l], ln1_b[l], alpha, l)
        x = _mlp(x, mods[l], w1_h, w2_h, ln2_g[l], ln2_b[l], alpha, l)
    return x
```
